```python
import jax, jax.numpy as jnp
from jax import lax
import numpy as np

D_MODEL = 1024
BATCH = 8
SEQ = 4096
DEPTH = 1

D_MIX = 1024
GDN_HEADS = 4
GDN_DK = 128
GDN_DV = 128
GDN_CONV = 4
GDN_CHUNK = 64
RWKV_HEADS = 8
RWKV_N = 64
LORA_W = 64
LORA_A = 64
LORA_G = 128
D_FF = 2816
D_PLE = 256
NORM_EPS = 1e-6
GN_EPS = 64e-5
L2_EPS = 1e-6

GDN_QK_W = GDN_HEADS * GDN_DK
GDN_W = GDN_HEADS * GDN_DV
GDN_CONV_COLS = 2 * GDN_QK_W + GDN_W
GDN_COLS = GDN_CONV_COLS + GDN_W + 2 * GDN_HEADS
RWKV_W = RWKV_HEADS * RWKV_N
RWKV_COLS = 3 * RWKV_W + LORA_W + LORA_A + LORA_G
IN_COLS = GDN_COLS + RWKV_COLS

kernel_name = 'hybrid_gdn_rwkv7_macaron_block'


def rmsnorm(x, g):
    xf = x.astype(jnp.float32)
    y = xf * lax.rsqrt(jnp.mean(xf * xf, axis=-1, keepdims=True) + NORM_EPS)
    return (y * g.astype(jnp.float32)).astype(x.dtype)


def l2norm(x):
    return x * lax.rsqrt(jnp.sum(x * x, axis=-1, keepdims=True) + L2_EPS)


def swiglu(x, w_gate, w_up, w_down):
    return (jax.nn.silu(x @ w_gate) * (x @ w_up)) @ w_down


def causal_depthwise_conv(x, w):
    k = w.shape[0]
    return lax.conv_general_dilated(
        x, w[:, None, :].astype(x.dtype), window_strides=(1,), padding=((k - 1, 0),),
        dimension_numbers=('NWC', 'WIO', 'NWC'), feature_group_count=x.shape[-1])


def to_chunks(t, n_chunks):
    b, _, h = t.shape[:3]
    t = t.reshape(b, n_chunks, GDN_CHUNK, h, *t.shape[3:])
    return jnp.moveaxis(t, 3, 1)


def gated_delta_rule_chunked(q, k, v, g, beta):
    b, t, h, dk = q.shape
    dv = v.shape[-1]
    nc = t // GDN_CHUNK
    q = to_chunks(q * dk ** -0.5, nc)
    k = to_chunks(k, nc)
    v = to_chunks(v, nc)
    g = to_chunks(g, nc)
    beta = to_chunks(beta, nc)
    gc = jnp.cumsum(g, axis=-1)
    idx = jnp.arange(GDN_CHUNK)
    causal = idx[:, None] >= idx[None, :]
    strict = idx[:, None] > idx[None, :]
    decay = jnp.where(causal, jnp.exp(jnp.where(causal, gc[..., :, None] - gc[..., None, :], 0.0)), 0.0)
    kb = k * beta[..., None]
    vb = v * beta[..., None]
    lmat = jnp.where(strict, jnp.einsum('bhnid,bhnjd->bhnij', kb, k) * decay, 0.0)
    eye = jnp.eye(GDN_CHUNK, dtype=jnp.float32)
    tinv = lax.linalg.triangular_solve(eye + lmat, jnp.broadcast_to(eye, lmat.shape),
                                       left_side=True, lower=True, unit_diagonal=True)
    u = tinv @ vb
    wk = tinv @ (kb * jnp.exp(gc)[..., None])
    a_qk = jnp.einsum('bhnid,bhnjd->bhnij', q, k) * decay
    q_dec = q * jnp.exp(gc)[..., None]
    k_dec = k * jnp.exp(gc[..., -1:] - gc)[..., None]
    g_last = jnp.exp(gc[..., -1])
    xs = tuple(jnp.moveaxis(a, 2, 0) for a in (u, wk, q_dec, a_qk, k_dec, g_last))

    def step(state, inp):
        u_n, w_n, q_n, a_n, k_n, gl_n = inp
        v_new = u_n - w_n @ state
        o_n = q_n @ state + a_n @ v_new
        state = state * gl_n[..., None, None] + jnp.swapaxes(k_n, -1, -2) @ v_new
        return state, o_n

    s0 = jnp.zeros((b, h, dk, dv), jnp.float32)
    _, o = lax.scan(step, s0, xs)
    return jnp.transpose(o, (1, 0, 3, 2, 4)).reshape(b, t, h, dv)


def rwkv7_scan(r, w, k, v, kk, a):
    b, _, h, n = r.shape
    xs = tuple(jnp.moveaxis(t, 1, 0) for t in (r, w, k, v, kk, a))

    def step(state, inp):
        r_t, w_t, k_t, v_t, kk_t, a_t = inp
        sa = jnp.einsum('bhvk,bhk->bhv', state, -kk_t)
        state = (state * w_t[:, :, None, :] + sa[..., None] * (kk_t * a_t)[:, :, None, :]
                 + v_t[..., None] * k_t[:, :, None, :])
        y = jnp.einsum('bhvk,bhk->bhv', state, r_t)
        return state, y

    s0 = jnp.zeros((b, h, n, n), jnp.float32)
    _, y = lax.scan(step, s0, xs)
    return jnp.moveaxis(y, 0, 1)


def hybrid_mixer(h, w_in, conv_w, a_log, dt_bias, gdn_norm_w, mu, w0, w2, a0, a2, g2,
                 k_k, k_a, r_k, gn_w, gn_b, w_out):
    b, t, _ = h.shape
    f32 = jnp.float32
    proj = (h @ w_in).astype(f32)

    gdn = proj[..., :GDN_COLS]
    qkv = jax.nn.silu(causal_depthwise_conv(gdn[..., :GDN_CONV_COLS], conv_w.astype(f32)))
    q = l2norm(qkv[..., :GDN_QK_W].reshape(b, t, GDN_HEADS, GDN_DK))
    k = l2norm(qkv[..., GDN_QK_W:2 * GDN_QK_W].reshape(b, t, GDN_HEADS, GDN_DK))
    v = qkv[..., 2 * GDN_QK_W:].reshape(b, t, GDN_HEADS, GDN_DV)
    z = gdn[..., GDN_CONV_COLS:GDN_CONV_COLS + GDN_W].reshape(b, t, GDN_HEADS, GDN_DV)
    off = GDN_CONV_COLS + GDN_W
    beta = jax.nn.sigmoid(gdn[..., off:off + GDN_HEADS])
    g = -jnp.exp(a_log.astype(f32)) * jax.nn.softplus(gdn[..., off + GDN_HEADS:] + dt_bias.astype(f32))
    o = gated_delta_rule_chunked(q, k, v, g, beta)
    o = rmsnorm(o, gdn_norm_w) * jax.nn.silu(z)
    y_gdn = o.reshape(b, t, GDN_W)

    rw = proj[..., GDN_COLS:]
    prev = jnp.pad(rw, ((0, 0), (1, 0), (0, 0)))[:, :-1]
    xs = rw + (prev - rw) * mu.astype(f32)
    r = xs[..., :RWKV_W]
    kr = xs[..., RWKV_W:2 * RWKV_W]
    vr = xs[..., 2 * RWKV_W:3 * RWKV_W]
    off = 3 * RWKV_W
    lw = xs[..., off:off + LORA_W]
    la = xs[..., off + LORA_W:off + LORA_W + LORA_A]
    lg = xs[..., off + LORA_W + LORA_A:]
    w_log = -jax.nn.softplus(-(w0.astype(f32) + jnp.tanh(lw) @ w2.astype(f32))) - 0.5
    decay = jnp.exp(-jnp.exp(w_log))
    a = jax.nn.sigmoid(a0.astype(f32) + la @ a2.astype(f32))
    gate = jax.nn.sigmoid(lg) @ g2.astype(f32)
    hs = (b, t, RWKV_HEADS, RWKV_N)
    r, kr, vr, a, decay = (u.reshape(hs) for u in (r, kr, vr, a, decay))
    kk = l2norm(kr * k_k.astype(f32).reshape(RWKV_HEADS, RWKV_N))
    kr = kr * (1.0 + (a - 1.0) * k_a.astype(f32).reshape(RWKV_HEADS, RWKV_N))
    y = rwkv7_scan(r, decay, kr, vr, kk, a)
    mean = jnp.mean(y, axis=-1, keepdims=True)
    var = jnp.mean(jnp.square(y - mean), axis=-1, keepdims=True)
    y = ((y - mean) * lax.rsqrt(var + GN_EPS) * gn_w.astype(f32).reshape(RWKV_HEADS, RWKV_N)
         + gn_b.astype(f32).reshape(RWKV_HEADS, RWKV_N))
    y = y + jnp.sum(r * kr * r_k.astype(f32), axis=-1, keepdims=True) * vr
    y_rwkv = y.reshape(b, t, RWKV_W) * gate

    merged = jnp.concatenate([y_gdn, y_rwkv], axis=-1).astype(h.dtype)
    return merged @ w_out


def setup_inputs(seed: int = 0) -> dict:
    key = jax.random.key(seed)
    ks = list(jax.random.split(key, 40))
    f32 = jnp.float32

    def nrm(shape, scale):
        return jax.random.normal(ks.pop(), shape, f32) * scale

    def gain(n):
        return 1.0 + 0.02 * jax.random.normal(ks.pop(), (DEPTH, n), f32)

    def unif(shape, lo, hi):
        return jax.random.uniform(ks.pop(), shape, f32, minval=lo, maxval=hi)

    dt = jnp.exp(unif((DEPTH, GDN_HEADS), float(np.log(1e-3)), float(np.log(1e-1))))
    inp = {}
    inp['x'] = nrm((BATCH, SEQ, D_MODEL), 1.0)
    inp['p'] = nrm((DEPTH, BATCH, SEQ, D_PLE), 1.0)
    inp['ffn1_norm_pre'] = gain(D_MODEL)
    inp['ffn1_w_gate'] = nrm((DEPTH, D_MODEL, D_FF), D_MODEL ** -0.5)
    inp['ffn1_w_up'] = nrm((DEPTH, D_MODEL, D_FF), D_MODEL ** -0.5)
    inp['ffn1_w_down'] = nrm((DEPTH, D_FF, D_MODEL), D_FF ** -0.5)
    inp['ffn1_norm_post'] = gain(D_MODEL)
    inp['mix_norm_pre'] = gain(D_MODEL)
    inp['w_in'] = nrm((DEPTH, D_MODEL, IN_COLS), D_MODEL ** -0.5)
    inp['gdn_conv_w'] = nrm((DEPTH, GDN_CONV, GDN_CONV_COLS), GDN_CONV ** -0.5)
    inp['gdn_a_log'] = jnp.log(unif((DEPTH, GDN_HEADS), 1.0, 16.0))
    inp['gdn_dt_bias'] = dt + jnp.log(-jnp.expm1(-dt))
    inp['gdn_norm_w'] = gain(GDN_DV)
    inp['rwkv_mu'] = unif((DEPTH, RWKV_COLS), 0.0, 1.0)
    inp['rwkv_w0'] = unif((DEPTH, RWKV_W), -6.0, 1.0)
    inp['rwkv_w2'] = nrm((DEPTH, LORA_W, RWKV_W), LORA_W ** -0.5)
    inp['rwkv_a0'] = nrm((DEPTH, RWKV_W), 0.5)
    inp['rwkv_a2'] = nrm((DEPTH, LORA_A, RWKV_W), LORA_A ** -0.5)
    inp['rwkv_g2'] = nrm((DEPTH, LORA_G, RWKV_W), LORA_G ** -0.5)
    inp['rwkv_k_k'] = 0.85 + nrm((DEPTH, RWKV_W), 0.05)
    inp['rwkv_k_a'] = 1.0 + nrm((DEPTH, RWKV_W), 0.05)
    inp['rwkv_r_k'] = nrm((DEPTH, RWKV_HEADS, RWKV_N), 0.1)
    inp['rwkv_gn_w'] = gain(RWKV_W)
    inp['rwkv_gn_b'] = nrm((DEPTH, RWKV_W), 0.01)
    inp['w_out'] = nrm((DEPTH, D_MIX, D_MODEL), D_MIX ** -0.5)
    inp['mix_norm_post'] = gain(D_MODEL)
    inp['ffn2_norm_pre'] = gain(D_MODEL)
    inp['ffn2_w_gate'] = nrm((DEPTH, D_MODEL, D_FF), D_MODEL ** -0.5)
    inp['ffn2_w_up'] = nrm((DEPTH, D_MODEL, D_FF), D_MODEL ** -0.5)
    inp['ffn2_w_down'] = nrm((DEPTH, D_FF, D_MODEL), D_FF ** -0.5)
    inp['ffn2_norm_post'] = gain(D_MODEL)
    inp['ple_norm_pre'] = gain(D_MODEL)
    inp['ple_w_gate'] = nrm((DEPTH, D_MODEL, D_MODEL), D_MODEL ** -0.5)
    inp['ple_w_proj'] = nrm((DEPTH, D_PLE, D_MODEL), D_PLE ** -0.5)
    inp['ple_norm_post'] = gain(D_MODEL)
    return inp


def reference(x, p, ffn1_norm_pre, ffn1_w_gate, ffn1_w_up, ffn1_w_down, ffn1_norm_post,
              mix_norm_pre, w_in, gdn_conv_w, gdn_a_log, gdn_dt_bias, gdn_norm_w,
              rwkv_mu, rwkv_w0, rwkv_w2, rwkv_a0, rwkv_a2, rwkv_g2, rwkv_k_k, rwkv_k_a, rwkv_r_k,
              rwkv_gn_w, rwkv_gn_b, w_out, mix_norm_post,
              ffn2_norm_pre, ffn2_w_gate, ffn2_w_up, ffn2_w_down, ffn2_norm_post,
              ple_norm_pre, ple_w_gate, ple_w_proj, ple_norm_post):
    for i in range(DEPTH):
        f = swiglu(rmsnorm(x, ffn1_norm_pre[i]), ffn1_w_gate[i], ffn1_w_up[i], ffn1_w_down[i])
        x = x + 0.5 * rmsnorm(f, ffn1_norm_post[i])
        m = hybrid_mixer(rmsnorm(x, mix_norm_pre[i]), w_in[i], gdn_conv_w[i], gdn_a_log[i],
                         gdn_dt_bias[i], gdn_norm_w[i], rwkv_mu[i], rwkv_w0[i], rwkv_w2[i],
                         rwkv_a0[i], rwkv_a2[i], rwkv_g2[i], rwkv_k_k[i], rwkv_k_a[i], rwkv_r_k[i],
                         rwkv_gn_w[i], rwkv_gn_b[i], w_out[i])
        x = x + rmsnorm(m, mix_norm_post[i])
        f = swiglu(rmsnorm(x, ffn2_norm_pre[i]), ffn2_w_gate[i], ffn2_w_up[i], ffn2_w_down[i])
        x = x + 0.5 * rmsnorm(f, ffn2_norm_post[i])
        gate = jax.nn.sigmoid(rmsnorm(x, ple_norm_pre[i]) @ ple_w_gate[i])
        x = x + rmsnorm(gate * (p[i] @ ple_w_proj[i]), ple_norm_post[i])
    return x
```

```python
import functools

import jax
import jax.numpy as jnp
from jax import lax
from jax.experimental import pallas as pl
from jax.experimental.pallas import tpu as pltpu

F32 = jnp.float32
BF16 = jnp.bfloat16

D_MODEL = 1024
D_FF = 2816
D_PLE = 256
GDN_HEADS = 4
GDN_D = 128
GDN_CONV = 4
RWKV_HEADS = 8
RWKV_N = 64
RWKV_W = RWKV_HEADS * RWKV_N
GDN_W = GDN_HEADS * GDN_D
LORA_WA = 128
LORA_G = 128
NORM_EPS = 1e-6
GN_EPS = 64e-5
L2_EPS = 1e-6

COL_QKV = 0
COL_Z = 3 * GDN_W
COL_RW = 4 * GDN_W
RW_COLS = 3 * RWKV_W + LORA_WA + LORA_G
COL_BG = COL_RW + RW_COLS
N_PROJ = COL_BG + 128

V7X_LANES = 128
GDN_CHUNK = 128
RWKV_CHUNK = 64
HALO = 8

FF_CHUNKS = ((0, 768), (768, 768), (1536, 768), (2304, 512))
VMEM_LIMIT = 56 * 1024 * 1024


def _dot(a, b):
    return jnp.dot(a.astype(BF16), b.astype(BF16), preferred_element_type=F32)


def _dot_nt(a, b):
    return lax.dot_general(a.astype(BF16), b.astype(BF16), (((1,), (1,)), ((), ())),
                           preferred_element_type=F32)


def _dot_tn(a, b):
    return lax.dot_general(a.astype(BF16), b.astype(BF16), (((0,), (0,)), ((), ())),
                           preferred_element_type=F32)


def _rms(x, g):
    return x * lax.rsqrt(jnp.mean(x * x, axis=-1, keepdims=True) + NORM_EPS) * g


def _sigmoid(x):
    return 1.0 / (1.0 + jnp.exp(-x))


def _silu(x):
    return x * _sigmoid(x)


def _softplus(x):
    return jnp.maximum(x, 0.0) + jnp.log(1.0 + jnp.exp(-jnp.abs(x)))


def _split3(x):
    hi = x.astype(BF16)
    r1 = x - hi.astype(F32)
    mid = r1.astype(BF16)
    lo = (r1 - mid.astype(F32)).astype(BF16)
    return hi, mid, lo


def _cumsum_rows(tri, x):
    hi, mid, lo = _split3(x)
    return (jnp.dot(tri, hi, preferred_element_type=F32)
            + jnp.dot(tri, mid, preferred_element_type=F32)
            + jnp.dot(tri, lo, preferred_element_type=F32))


def _unit_lower_inverse(lmat, n_doublings):
    n = lmat.shape[0]
    row = lax.broadcasted_iota(jnp.int32, (n, n), 0)
    col = lax.broadcasted_iota(jnp.int32, (n, n), 1)
    p = jnp.where(row == col, 1.0, 0.0) - lmat
    lp = lmat
    for _ in range(n_doublings):
        lp = _dot(lp, lp)
        p = p + _dot(p, lp)
    return p


def _swiglu_ffn(x, gpre, wg_ref, wu_ref, wd_ref, gpost, a_scr):
    h = _rms(x, gpre).astype(BF16)
    for s, w in FF_CHUNKS:
        g = jnp.dot(h, wg_ref[:, s:s + w], preferred_element_type=F32)
        u = jnp.dot(h, wu_ref[:, s:s + w], preferred_element_type=F32)
        a_scr[:, s:s + w] = (_silu(g) * u).astype(BF16)
    f = jnp.dot(a_scr[...], wd_ref[...], preferred_element_type=F32)
    return x + 0.5 * _rms(f, gpost)


def _const_spec(shape):
    return pl.BlockSpec(shape, lambda *_: (0,) * len(shape), pipeline_mode=pl.Buffered(1))


def _ffn_kernel(x_ref, gpre_ref, wg_ref, wu_ref, wd_ref, gpost_ref, o_ref, a_scr):
    o_ref[...] = _swiglu_ffn(x_ref[...], gpre_ref[...], wg_ref, wu_ref, wd_ref, gpost_ref[...], a_scr)


def _ffn(x2d, gpre, wg, wu, wd, gpost, tm=512):
    n = x2d.shape[0]
    row = pl.BlockSpec((tm, D_MODEL), lambda i: (i, 0))
    return pl.pallas_call(
        _ffn_kernel,
        out_shape=jax.ShapeDtypeStruct(x2d.shape, F32),
        grid=(n // tm,),
        in_specs=[row, _const_spec((1, D_MODEL)), _const_spec((D_MODEL, D_FF)), _const_spec((D_MODEL, D_FF)),
                  _const_spec((D_FF, D_MODEL)), _const_spec((1, D_MODEL))],
        out_specs=row,
        scratch_shapes=[pltpu.VMEM((tm, D_FF), BF16)],
        compiler_params=pltpu.CompilerParams(dimension_semantics=("arbitrary",), vmem_limit_bytes=VMEM_LIMIT),
        name="ffn",
    )(x2d, gpre, wg, wu, wd, gpost)


def _proj_kernel(x_ref, gmix_ref, win_ref, convw_ref, alog_ref, dtb_ref, mu_ref, w0_ref, a0_ref, kk_ref, ka_ref,
                 w2a2_ref, g2_ref,
                 q_ref, k_ref, v_ref, z_ref, bg_ref, r_ref, lw_ref, rk_ref, rv_ref, rkk_ref, ra_ref, gate_ref,
                 pbuf):
    tm = x_ref.shape[0]

    @pl.when(pl.program_id(1) == 0)
    def _():
        pbuf[0:HALO, :] = jnp.zeros((HALO, N_PROJ), F32)

    h = _rms(x_ref[...], gmix_ref[...]).astype(BF16)
    pbuf[HALO:HALO + tm, :] = jnp.dot(h, win_ref[...], preferred_element_type=F32)

    lane = lax.broadcasted_iota(jnp.int32, (tm, V7X_LANES), 1)
    lo_half = lane < RWKV_N

    for grp, out_ref in ((0, q_ref), (1, k_ref), (2, v_ref)):
        for hd in range(GDN_HEADS):
            c0 = COL_QKV + grp * GDN_W + hd * GDN_D
            acc = None
            for j in range(GDN_CONV):
                tap = pbuf[HALO - (GDN_CONV - 1) + j:HALO - (GDN_CONV - 1) + j + tm, c0:c0 + GDN_D]
                term = tap * convw_ref[j:j + 1, c0:c0 + GDN_D]
                acc = term if acc is None else acc + term
            y = _silu(acc)
            if grp < 2:
                y = y * lax.rsqrt(jnp.sum(y * y, axis=-1, keepdims=True) + L2_EPS)
            if grp == 0:
                y = y * (GDN_D ** -0.5)
            out_ref[:, hd * GDN_D:(hd + 1) * GDN_D] = y
    z_ref[...] = pbuf[HALO:HALO + tm, COL_Z:COL_Z + GDN_W]

    pb = pbuf[HALO:HALO + tm, COL_BG:COL_BG + V7X_LANES]
    beta = _sigmoid(pb)
    gdec = -jnp.exp(alog_ref[...]) * _softplus(pb + dtb_ref[...])
    bg_ref[...] = jnp.where(lane < GDN_HEADS, beta, jnp.where(lane < 2 * GDN_HEADS, gdec, 0.0))

    def shifted(c0, width):
        cur = pbuf[HALO:HALO + tm, COL_RW + c0:COL_RW + c0 + width]
        prev = pbuf[HALO - 1:HALO - 1 + tm, COL_RW + c0:COL_RW + c0 + width]
        return cur + (prev - cur) * mu_ref[:, c0:c0 + width]

    lwa = shifted(3 * RWKV_W, LORA_WA)
    lwa = jnp.where(lo_half, jnp.tanh(lwa), lwa)
    wa = jnp.dot(lwa.astype(BF16), w2a2_ref[...], preferred_element_type=F32)
    lg = shifted(3 * RWKV_W + LORA_WA, LORA_G)
    gate_ref[...] = jnp.dot(_sigmoid(lg).astype(BF16), g2_ref[...], preferred_element_type=F32)
    for p in range(RWKV_W // V7X_LANES):
        sl = slice(p * V7X_LANES, (p + 1) * V7X_LANES)
        w_log = -_softplus(-(w0_ref[:, sl] + wa[:, sl])) - 0.5
        lw_ref[:, sl] = -jnp.exp(w_log)
        a = _sigmoid(a0_ref[:, sl] + wa[:, RWKV_W + p * V7X_LANES:RWKV_W + (p + 1) * V7X_LANES])
        ra_ref[:, sl] = a
        r_ref[:, sl] = shifted(p * V7X_LANES, V7X_LANES)
        kr = shifted(RWKV_W + p * V7X_LANES, V7X_LANES)
        rv_ref[:, sl] = shifted(2 * RWKV_W + p * V7X_LANES, V7X_LANES)
        kx = kr * kk_ref[:, sl]
        sq = kx * kx
        s_lo = jnp.sum(jnp.where(lo_half, sq, 0.0), axis=-1, keepdims=True)
        s_hi = jnp.sum(jnp.where(lo_half, 0.0, sq), axis=-1, keepdims=True)
        rkk_ref[:, sl] = kx * lax.rsqrt(jnp.where(lo_half, s_lo, s_hi) + L2_EPS)
        rk_ref[:, sl] = kr * (1.0 + (a - 1.0) * ka_ref[:, sl])

    pbuf[0:HALO, :] = pbuf[tm:tm + HALO, :]


def _proj_prep(x1, gmix, win, convw, alog, dtb, mu, w0, a0, k_k, k_a, w2a2, g2, tm=256):
    b, t, _ = x1.shape
    wide = jax.ShapeDtypeStruct((b, t, GDN_W), F32)
    narrow = jax.ShapeDtypeStruct((b, t, V7X_LANES), F32)
    spec_w = pl.BlockSpec((None, tm, GDN_W), lambda i, j: (i, j, 0))
    spec_n = pl.BlockSpec((None, tm, V7X_LANES), lambda i, j: (i, j, 0))
    consts = (gmix, win, convw, alog, dtb, mu, w0, a0, k_k, k_a, w2a2, g2)
    return pl.pallas_call(
        _proj_kernel,
        out_shape=(wide, wide, wide, wide, narrow, wide, wide, wide, wide, wide, wide, wide),
        grid=(b, t // tm),
        in_specs=[pl.BlockSpec((None, tm, D_MODEL), lambda i, j: (i, j, 0))] + [_const_spec(c.shape) for c in consts],
        out_specs=(spec_w, spec_w, spec_w, spec_w, spec_n, spec_w, spec_w, spec_w, spec_w, spec_w, spec_w, spec_w),
        scratch_shapes=[pltpu.VMEM((tm + HALO, N_PROJ), F32)],
        compiler_params=pltpu.CompilerParams(dimension_semantics=("arbitrary", "arbitrary"),
                                             vmem_limit_bytes=VMEM_LIMIT),
        name="proj_prep",
    )(x1, *consts)


def _gdn_kernel(q_ref, k_ref, v_ref, z_ref, bg_ref, nw_ref, y_ref, s_ref):
    tt = q_ref.shape[0]
    c = GDN_CHUNK

    @pl.when(pl.program_id(1) == 0)
    def _():
        s_ref[...] = jnp.zeros(s_ref.shape, F32)

    row = lax.broadcasted_iota(jnp.int32, (c, c), 0)
    col = lax.broadcasted_iota(jnp.int32, (c, c), 1)
    incl = row >= col
    strict = row > col
    tri = jnp.where(incl, 1.0, 0.0).astype(BF16)

    def chunk(ci, carry):
        rows = pl.ds(pl.multiple_of(ci * c, c), c)
        bg = bg_ref[rows, :]
        gc = _cumsum_rows(tri, bg)
        gct = gc.T
        for hd in range(GDN_HEADS):
            sl = slice(hd * GDN_D, (hd + 1) * GDN_D)
            beta = bg[:, hd:hd + 1]
            gcol = gc[:, GDN_HEADS + hd:GDN_HEADS + hd + 1]
            grow = gct[GDN_HEADS + hd:GDN_HEADS + hd + 1, :]
            glast = gc[c - 1:c, GDN_HEADS + hd:GDN_HEADS + hd + 1]
            dec = jnp.where(incl, jnp.exp(jnp.where(incl, gcol - grow, 0.0)), 0.0)
            q = q_ref[rows, sl]
            k = k_ref[rows, sl]
            v = v_ref[rows, sl]
            kb = k * beta
            vb = v * beta
            kq = _dot_nt(jnp.concatenate([kb, q], axis=0), k)
            lmat = jnp.where(strict, kq[:c] * dec, 0.0)
            a_qk = kq[c:] * dec
            tinv = _unit_lower_inverse(lmat, 6)
            eg = jnp.exp(gcol)
            uw = _dot(tinv, jnp.concatenate([vb, kb * eg], axis=1))
            u = uw[:, :GDN_D]
            wk = uw[:, GDN_D:]
            state = s_ref[hd]
            zs = _dot(jnp.concatenate([wk, q * eg], axis=0), state)
            v_new = u - zs[:c]
            o = zs[c:] + _dot(a_qk, v_new)
            s_ref[hd] = state * jnp.exp(glast) + _dot_tn(k * jnp.exp(glast - gcol), v_new)
            o = o * lax.rsqrt(jnp.mean(o * o, axis=-1, keepdims=True) + NORM_EPS) * nw_ref[...]
            y_ref[rows, sl] = o * _silu(z_ref[rows, sl])
        return carry

    lax.fori_loop(0, tt // c, chunk, 0)


def _gdn(q, k, v, z, bg, nw, tt=512):
    b, t, _ = q.shape
    spec_w = pl.BlockSpec((None, tt, GDN_W), lambda i, j: (i, j, 0))
    spec_n = pl.BlockSpec((None, tt, V7X_LANES), lambda i, j: (i, j, 0))
    return pl.pallas_call(
        _gdn_kernel,
        out_shape=jax.ShapeDtypeStruct((b, t, GDN_W), F32),
        grid=(b, t // tt),
        in_specs=[spec_w, spec_w, spec_w, spec_w, spec_n, _const_spec((1, GDN_D))],
        out_specs=spec_w,
        scratch_shapes=[pltpu.VMEM((GDN_HEADS, GDN_D, GDN_D), F32)],
        compiler_params=pltpu.CompilerParams(dimension_semantics=("arbitrary", "arbitrary"),
                                             vmem_limit_bytes=VMEM_LIMIT),
        name="gdn",
    )(q, k, v, z, bg, nw)


def _rwkv_kernel(r_ref, lw_ref, k_ref, v_ref, kk_ref, a_ref, gate_ref, rk_ref, gnw_ref, gnb_ref, y_ref, s_ref):
    tt = r_ref.shape[0]
    c = RWKV_CHUNK
    n = RWKV_N

    @pl.when(pl.program_id(1) == 0)
    def _():
        s_ref[...] = jnp.zeros(s_ref.shape, F32)

    row_c = lax.broadcasted_iota(jnp.int32, (c, c), 0)
    col_c = lax.broadcasted_iota(jnp.int32, (c, c), 1)
    tri = jnp.where(row_c >= col_c, 1.0, 0.0).astype(BF16)
    row = lax.broadcasted_iota(jnp.int32, (c, 2 * n), 0)
    lane = lax.broadcasted_iota(jnp.int32, (c, 2 * n), 1)
    m0 = lane < n
    jj = jnp.where(m0, lane, lane - n)
    strict = row > jj
    incl = row >= jj
    row2 = lax.broadcasted_iota(jnp.int32, (2 * n, 2 * n), 0)
    lane2 = lax.broadcasted_iota(jnp.int32, (2 * n, 2 * n), 1)
    same_head = (row2 < n) == (lane2 < n)

    def sel0(x):
        return jnp.where(m0, x, 0.0)

    def sel1(x):
        return jnp.where(m0, 0.0, x)

    def chunk(ci, carry):
        rows = pl.ds(pl.multiple_of(ci * c, c), c)
        g_all = _cumsum_rows(tri, lw_ref[rows, :])
        for p in range(RWKV_HEADS // 2):
            sl = slice(p * 2 * n, (p + 1) * 2 * n)
            lw = lw_ref[rows, sl]
            r = r_ref[rows, sl]
            k = k_ref[rows, sl]
            v = v_ref[rows, sl]
            kk = kk_ref[rows, sl]
            a = a_ref[rows, sl]
            g = g_all[:, sl]
            gmid = g[c // 2 - 1:c // 2, :]
            gend = g[c - 1:c, :]
            e_r = jnp.exp(g - gmid)
            e_kap = jnp.exp(g - lw - gmid)
            e_inv = jnp.exp(gmid - g)
            e_end = jnp.exp(gend - g)
            egm = jnp.exp(gmid)
            b = kk * a
            kap_t = kk * e_kap
            r_t = r * e_r
            b_t = b * e_inv
            k_t = k * e_inv
            a0 = _dot_nt(jnp.concatenate([sel0(kap_t), sel0(r_t)], axis=0), jnp.concatenate([b_t, k_t], axis=0))
            a1 = _dot_nt(jnp.concatenate([sel1(kap_t), sel1(r_t)], axis=0), jnp.concatenate([k_t, b_t], axis=0))
            lmat = jnp.concatenate([jnp.where(strict & m0, a0[:c], 0.0),
                                    jnp.where(strict & (~m0), a1[:c], 0.0)], axis=0)
            tbd = _unit_lower_inverse(lmat, 5)
            tcat = tbd[:c] + tbd[c:]
            a_uk = jnp.where(strict, jnp.where(m0, a1[:c], a0[:c]), 0.0)
            a_rk = jnp.where(incl, jnp.where(m0, a1[c:], a0[c:]), 0.0)
            a_rb = jnp.where(incl, jnp.where(m0, a0[c:], a1[c:]), 0.0)
            v_sw = jnp.concatenate([sel1(v), sel0(v)], axis=0)
            avy = _dot(jnp.concatenate([a_uk, a_rk], axis=0), v_sw)
            av = avy[:c]
            yv = avy[c:]
            kap_b = kap_t * egm
            x = jnp.concatenate([jnp.concatenate([sel0(kap_b), sel0(av)], axis=1),
                                 jnp.concatenate([sel1(kap_b), sel1(av)], axis=1)], axis=0)
            wu = _dot(tcat, x)
            w_b = wu[:, :2 * n]
            u_t = wu[:, 2 * n:]
            state = s_ref[p]
            zs = _dot_nt(jnp.concatenate([w_b, r_t * egm], axis=0), state)
            u = zs[:c] + u_t
            y = zs[c:] - _dot(a_rb, jnp.concatenate([sel0(u), sel1(u)], axis=0)) + yv
            upd = _dot_tn(jnp.concatenate([v, u], axis=0), jnp.concatenate([k * e_end, -(b * e_end)], axis=0))
            s_ref[p] = state * jnp.exp(gend) + jnp.where(same_head, upd, 0.0)

            def head_sum(t):
                s_lo = jnp.sum(sel0(t), axis=-1, keepdims=True)
                s_hi = jnp.sum(sel1(t), axis=-1, keepdims=True)
                return jnp.where(m0, s_lo, s_hi)

            mean = head_sum(y) * (1.0 / n)
            dlt = y - mean
            var = head_sum(dlt * dlt) * (1.0 / n)
            yn = dlt * lax.rsqrt(var + GN_EPS) * gnw_ref[:, sl] + gnb_ref[:, sl]
            bonus = head_sum(r * k * rk_ref[:, sl]) * v
            y_ref[rows, sl] = (yn + bonus) * gate_ref[rows, sl]
        return carry

    lax.fori_loop(0, tt // c, chunk, 0)


def _rwkv(r, lw, k, v, kk, a, gate, r_k, gn_w, gn_b, tt=256):
    b, t, _ = r.shape
    spec_w = pl.BlockSpec((None, tt, RWKV_W), lambda i, j: (i, j, 0))
    cs = _const_spec((1, RWKV_W))
    return pl.pallas_call(
        _rwkv_kernel,
        out_shape=jax.ShapeDtypeStruct((b, t, RWKV_W), F32),
        grid=(b, t // tt),
        in_specs=[spec_w] * 7 + [cs, cs, cs],
        out_specs=spec_w,
        scratch_shapes=[pltpu.VMEM((RWKV_HEADS // 2, 2 * RWKV_N, 2 * RWKV_N), F32)],
        compiler_params=pltpu.CompilerParams(dimension_semantics=("arbitrary", "arbitrary"),
                                             vmem_limit_bytes=VMEM_LIMIT),
        name="rwkv",
    )(r, lw, k, v, kk, a, gate, r_k, gn_w, gn_b)


def _tail_kernel(x_ref, yg_ref, yr_ref, p_ref, wout_ref, gmp_ref, gpre_ref, wg_ref, wu_ref, wd_ref, gpost_ref,
                 gple_ref, wpg_ref, wpp_ref, gplep_ref, o_ref, a_scr):
    m = (jnp.dot(yg_ref[...].astype(BF16), wout_ref[0:GDN_W, :], preferred_element_type=F32)
         + jnp.dot(yr_ref[...].astype(BF16), wout_ref[GDN_W:GDN_W + RWKV_W, :], preferred_element_type=F32))
    x = x_ref[...] + _rms(m, gmp_ref[...])
    x = _swiglu_ffn(x, gpre_ref[...], wg_ref, wu_ref, wd_ref, gpost_ref[...], a_scr)
    gate = _sigmoid(jnp.dot(_rms(x, gple_ref[...]).astype(BF16), wpg_ref[...], preferred_element_type=F32))
    emb = jnp.dot(p_ref[...].astype(BF16), wpp_ref[...], preferred_element_type=F32)
    o_ref[...] = x + _rms(gate * emb, gplep_ref[...])


def _tail(x1, yg, yr, p, wout, gmp, gpre, wg, wu, wd, gpost, gple, wpg, wpp, gplep, tm=256):
    n = x1.shape[0]
    row = pl.BlockSpec((tm, D_MODEL), lambda i: (i, 0))
    half = pl.BlockSpec((tm, GDN_W), lambda i: (i, 0))
    ple = pl.BlockSpec((tm, D_PLE), lambda i: (i, 0))
    consts = (wout, gmp, gpre, wg, wu, wd, gpost, gple, wpg, wpp, gplep)
    return pl.pallas_call(
        _tail_kernel,
        out_shape=jax.ShapeDtypeStruct(x1.shape, F32),
        grid=(n // tm,),
        in_specs=[row, half, half, ple] + [_const_spec(c.shape) for c in consts],
        out_specs=row,
        scratch_shapes=[pltpu.VMEM((tm, D_FF), BF16)],
        compiler_params=pltpu.CompilerParams(dimension_semantics=("arbitrary",), vmem_limit_bytes=VMEM_LIMIT),
        name="tail",
    )(x1, yg, yr, p, *consts)


def _row(v):
    return v.reshape(1, -1).astype(F32)


def _layer(x, p, ffn1_norm_pre, ffn1_w_gate, ffn1_w_up, ffn1_w_down, ffn1_norm_post, mix_norm_pre, w_in,
           gdn_conv_w, gdn_a_log, gdn_dt_bias, gdn_norm_w, rwkv_mu, rwkv_w0, rwkv_w2, rwkv_a0, rwkv_a2, rwkv_g2,
           rwkv_k_k, rwkv_k_a, rwkv_r_k, rwkv_gn_w, rwkv_gn_b, w_out, mix_norm_post, ffn2_norm_pre, ffn2_w_gate,
           ffn2_w_up, ffn2_w_down, ffn2_norm_post, ple_norm_pre, ple_w_gate, ple_w_proj, ple_norm_post):
    b, t, d = x.shape
    n = b * t
    x1 = _ffn(x.reshape(n, d), _row(ffn1_norm_pre), ffn1_w_gate.astype(BF16), ffn1_w_up.astype(BF16),
              ffn1_w_down.astype(BF16), _row(ffn1_norm_post))

    n_gdn_main = 4 * GDN_W
    n_bg = 2 * GDN_HEADS
    win = jnp.concatenate([w_in[:, :n_gdn_main], w_in[:, n_gdn_main + n_bg:], w_in[:, n_gdn_main:n_gdn_main + n_bg],
                           jnp.zeros((d, V7X_LANES - n_bg), w_in.dtype)], axis=1).astype(BF16)
    pad_bg = lambda vec: jnp.concatenate([jnp.zeros((GDN_HEADS,), F32), vec.astype(F32),
                                          jnp.zeros((V7X_LANES - n_bg,), F32)]).reshape(1, V7X_LANES)
    zero_l = jnp.zeros((RWKV_N, RWKV_W), F32)
    w2a2 = jnp.concatenate([jnp.concatenate([rwkv_w2.astype(F32), zero_l], axis=1),
                            jnp.concatenate([zero_l, rwkv_a2.astype(F32)], axis=1)], axis=0).astype(BF16)
    outs = _proj_prep(x1.reshape(b, t, d), _row(mix_norm_pre), win, gdn_conv_w.astype(F32), pad_bg(gdn_a_log),
                      pad_bg(gdn_dt_bias), _row(rwkv_mu), _row(rwkv_w0), _row(rwkv_a0), _row(rwkv_k_k),
                      _row(rwkv_k_a), w2a2, rwkv_g2.astype(BF16))
    gq, gk, gv, gz, gbg, rr, rlw, rk, rv, rkk, ra, rgate = outs
    y_gdn = _gdn(gq, gk, gv, gz, gbg, _row(gdn_norm_w))
    y_rwkv = _rwkv(rr, rlw, rk, rv, rkk, ra, rgate, _row(rwkv_r_k), _row(rwkv_gn_w), _row(rwkv_gn_b))
    out = _tail(x1, y_gdn.reshape(n, GDN_W), y_rwkv.reshape(n, RWKV_W), p.reshape(n, D_PLE), w_out.astype(BF16),
                _row(mix_norm_post), _row(ffn2_norm_pre), ffn2_w_gate.astype(BF16), ffn2_w_up.astype(BF16),
                ffn2_w_down.astype(BF16), _row(ffn2_norm_post), _row(ple_norm_pre), ple_w_gate.astype(BF16),
                ple_w_proj.astype(BF16), _row(ple_norm_post))
    return out.reshape(b, t, d)


def kernel(x, p, ffn1_norm_pre, ffn1_w_gate, ffn1_w_up, ffn1_w_down, ffn1_norm_post, mix_norm_pre, w_in, gdn_conv_w, gdn_a_log, gdn_dt_bias, gdn_norm_w, rwkv_mu, rwkv_w0, rwkv_w2, rwkv_a0, rwkv_a2, rwkv_g2, rwkv_k_k, rwkv_k_a, rwkv_r_k, rwkv_gn_w, rwkv_gn_b, w_out, mix_norm_post, ffn2_norm_pre, ffn2_w_gate, ffn2_w_up, ffn2_w_down, ffn2_norm_post, ple_norm_pre, ple_w_gate, ple_w_proj, ple_norm_post):
    return _layer(x, p[0], ffn1_norm_pre[0], ffn1_w_gate[0], ffn1_w_up[0], ffn1_w_down[0], ffn1_norm_post[0],
                  mix_norm_pre[0], w_in[0], gdn_conv_w[0], gdn_a_log[0], gdn_dt_bias[0], gdn_norm_w[0], rwkv_mu[0],
                  rwkv_w0[0], rwkv_w2[0], rwkv_a0[0], rwkv_a2[0], rwkv_g2[0], rwkv_k_k[0], rwkv_k_a[0], rwkv_r_k[0],
                  rwkv_gn_w[0], rwkv_gn_b[0], w_out[0], mix_norm_post[0], ffn2_norm_pre[0], ffn2_w_gate[0],
                  ffn2_w_up[0], ffn2_w_down[0], ffn2_norm_post[0], ple_norm_pre[0], ple_w_gate[0], ple_w_proj[0],
                  ple_norm_post[0])
```

```python
import functools

import jax
import jax.numpy as jnp
from jax import lax
from jax.experimental import pallas as pl
from jax.experimental.pallas import tpu as pltpu

F32 = jnp.float32
BF16 = jnp.bfloat16

D_MODEL = 1024
D_FF = 2816
D_PLE = 256
GDN_HEADS = 4
GDN_D = 128
GDN_CONV = 4
RWKV_HEADS = 8
RWKV_N = 64
RWKV_W = RWKV_HEADS * RWKV_N
GDN_W = GDN_HEADS * GDN_D
LORA_WA = 128
LORA_G = 128
NORM_EPS = 1e-6
GN_EPS = 64e-5
L2_EPS = 1e-6

COL_QKV = 0
COL_Z = 3 * GDN_W
COL_RW = 4 * GDN_W
RW_COLS = 3 * RWKV_W + LORA_WA + LORA_G
COL_BG = COL_RW + RW_COLS
N_PROJ = COL_BG + 128

V7X_LANES = 128
GDN_CHUNK = 128
RWKV_CHUNK = 64
GDN_GROUP = 2
RWKV_GROUP = 2
HALO = 8

FF_CHUNKS = ((0, 768), (768, 768), (1536, 768), (2304, 512))
VMEM_LIMIT = 56 * 1024 * 1024


def _dot(a, b):
    return jnp.dot(a.astype(BF16), b.astype(BF16), preferred_element_type=F32)


def _dot_nt(a, b):
    return lax.dot_general(a.astype(BF16), b.astype(BF16), (((1,), (1,)), ((), ())),
                           preferred_element_type=F32)


def _rms(x, g):
    return x * lax.rsqrt(jnp.mean(x * x, axis=-1, keepdims=True) + NORM_EPS) * g


def _sigmoid(x):
    return 1.0 / (1.0 + jnp.exp(-x))


def _silu(x):
    return x * _sigmoid(x)


def _softplus(x):
    return jnp.maximum(x, 0.0) + jnp.log(1.0 + jnp.exp(-jnp.abs(x)))


def _split3(x):
    hi = x.astype(BF16)
    r1 = x - hi.astype(F32)
    mid = r1.astype(BF16)
    lo = (r1 - mid.astype(F32)).astype(BF16)
    return hi, mid, lo


def _cumsum_rows(tri, x):
    hi, mid, lo = _split3(x)
    return (jnp.dot(tri, hi, preferred_element_type=F32)
            + jnp.dot(tri, mid, preferred_element_type=F32)
            + jnp.dot(tri, lo, preferred_element_type=F32))


def _odd_blocks(x, b):
    return jnp.concatenate([x[s:s + b] for s in range(b, x.shape[0], 2 * b)], axis=0)


def _with_odd_blocks(base, odd, b):
    parts = []
    for j, s in enumerate(range(0, base.shape[0], 2 * b)):
        parts.append(base[s:s + b])
        parts.append(odd[j * b:(j + 1) * b])
    return jnp.concatenate(parts, axis=0)


def _unit_lower_inverses(lmats, top):
    n = lmats[0].shape[0]
    row = lax.broadcasted_iota(jnp.int32, (n, n), 0)
    col = lax.broadcasted_iota(jnp.int32, (n, n), 1)
    lower = row > col
    eye = jnp.where(row == col, 1.0, 0.0)
    xs = [eye - jnp.where(lower & ((row // 2) == (col // 2)), lm, 0.0) for lm in lmats]
    b = 2
    while b < top:
        sub = lower & ((row // (2 * b)) == (col // (2 * b))) & ((row // b) != (col // b))
        cs = [jnp.where(sub, lm, 0.0) for lm in lmats]
        if b % 8:
            ys = [_dot(c, x) for c, x in zip(cs, xs)]
            xs = [x - _dot(x, y) for x, y in zip(xs, ys)]
        else:
            zero = jnp.zeros((n, n), F32)
            ys = [_with_odd_blocks(zero, _dot(_odd_blocks(c, b), x), b) for c, x in zip(cs, xs)]
            xs = [_with_odd_blocks(x, _odd_blocks(x, b) - _dot(_odd_blocks(x, b), y), b) for x, y in zip(xs, ys)]
        b *= 2
    return xs


def _swiglu_ffn(x, gpre, wg_ref, wu_ref, wd_ref, gpost, a_scr):
    h = _rms(x, gpre).astype(BF16)
    for s, w in FF_CHUNKS:
        g = jnp.dot(h, wg_ref[:, s:s + w], preferred_element_type=F32)
        u = jnp.dot(h, wu_ref[:, s:s + w], preferred_element_type=F32)
        a_scr[:, s:s + w] = (_silu(g) * u).astype(BF16)
    f = jnp.dot(a_scr[...], wd_ref[...], preferred_element_type=F32)
    return x + 0.5 * _rms(f, gpost)


def _const_spec(shape):
    return pl.BlockSpec(shape, lambda *_: (0,) * len(shape), pipeline_mode=pl.Buffered(1))


def _ffn_kernel(x_ref, gpre_ref, wg_ref, wu_ref, wd_ref, gpost_ref, o_ref, a_scr):
    o_ref[...] = _swiglu_ffn(x_ref[...], gpre_ref[...], wg_ref, wu_ref, wd_ref, gpost_ref[...], a_scr)


def _ffn(x2d, gpre, wg, wu, wd, gpost, tm=512):
    n = x2d.shape[0]
    row = pl.BlockSpec((tm, D_MODEL), lambda i: (i, 0))
    return pl.pallas_call(
        _ffn_kernel,
        out_shape=jax.ShapeDtypeStruct(x2d.shape, F32),
        grid=(n // tm,),
        in_specs=[row, _const_spec((1, D_MODEL)), _const_spec((D_MODEL, D_FF)), _const_spec((D_MODEL, D_FF)),
                  _const_spec((D_FF, D_MODEL)), _const_spec((1, D_MODEL))],
        out_specs=row,
        scratch_shapes=[pltpu.VMEM((tm, D_FF), BF16)],
        compiler_params=pltpu.CompilerParams(dimension_semantics=("arbitrary",), vmem_limit_bytes=VMEM_LIMIT),
        name="ffn",
    )(x2d, gpre, wg, wu, wd, gpost)


def _proj_kernel(x_ref, gmix_ref, win_ref, convw_ref, alog_ref, dtb_ref, mu_ref, w0_ref, a0_ref, kk_ref, ka_ref,
                 w2a2_ref, g2_ref,
                 q_ref, k_ref, v_ref, z_ref, bg_ref, r_ref, lw_ref, rk_ref, rv_ref, rkk_ref, ra_ref, gate_ref,
                 pbuf):
    tm = x_ref.shape[0]

    @pl.when(pl.program_id(1) == 0)
    def _():
        pbuf[0:HALO, :] = jnp.zeros((HALO, N_PROJ), F32)

    h = _rms(x_ref[...], gmix_ref[...]).astype(BF16)
    pbuf[HALO:HALO + tm, :] = jnp.dot(h, win_ref[...], preferred_element_type=F32)

    lane = lax.broadcasted_iota(jnp.int32, (tm, V7X_LANES), 1)
    lo_half = lane < RWKV_N

    for grp, out_ref in ((0, q_ref), (1, k_ref), (2, v_ref)):
        for hd in range(GDN_HEADS):
            c0 = COL_QKV + grp * GDN_W + hd * GDN_D
            acc = None
            for j in range(GDN_CONV):
                tap = pbuf[HALO - (GDN_CONV - 1) + j:HALO - (GDN_CONV - 1) + j + tm, c0:c0 + GDN_D]
                term = tap * convw_ref[j:j + 1, c0:c0 + GDN_D]
                acc = term if acc is None else acc + term
            y = _silu(acc)
            if grp < 2:
                y = y * lax.rsqrt(jnp.sum(y * y, axis=-1, keepdims=True) + L2_EPS)
            if grp == 0:
                y = y * (GDN_D ** -0.5)
            out_ref[:, hd * GDN_D:(hd + 1) * GDN_D] = y
    z_ref[...] = pbuf[HALO:HALO + tm, COL_Z:COL_Z + GDN_W]

    pb = pbuf[HALO:HALO + tm, COL_BG:COL_BG + V7X_LANES]
    beta = _sigmoid(pb)
    gdec = -jnp.exp(alog_ref[...]) * _softplus(pb + dtb_ref[...])
    bg_ref[...] = jnp.where(lane < GDN_HEADS, beta, jnp.where(lane < 2 * GDN_HEADS, gdec, 0.0))

    def shifted(c0, width):
        cur = pbuf[HALO:HALO + tm, COL_RW + c0:COL_RW + c0 + width]
        prev = pbuf[HALO - 1:HALO - 1 + tm, COL_RW + c0:COL_RW + c0 + width]
        return cur + (prev - cur) * mu_ref[:, c0:c0 + width]

    lwa = shifted(3 * RWKV_W, LORA_WA)
    lwa = jnp.where(lo_half, jnp.tanh(lwa), lwa)
    wa = jnp.dot(lwa.astype(BF16), w2a2_ref[...], preferred_element_type=F32)
    lg = shifted(3 * RWKV_W + LORA_WA, LORA_G)
    gate_ref[...] = jnp.dot(_sigmoid(lg).astype(BF16), g2_ref[...], preferred_element_type=F32)
    for p in range(RWKV_W // V7X_LANES):
        sl = slice(p * V7X_LANES, (p + 1) * V7X_LANES)
        w_log = -_softplus(-(w0_ref[:, sl] + wa[:, sl])) - 0.5
        lw_ref[:, sl] = -jnp.exp(w_log)
        a = _sigmoid(a0_ref[:, sl] + wa[:, RWKV_W + p * V7X_LANES:RWKV_W + (p + 1) * V7X_LANES])
        ra_ref[:, sl] = a
        r_ref[:, sl] = shifted(p * V7X_LANES, V7X_LANES)
        kr = shifted(RWKV_W + p * V7X_LANES, V7X_LANES)
        rv_ref[:, sl] = shifted(2 * RWKV_W + p * V7X_LANES, V7X_LANES)
        kx = kr * kk_ref[:, sl]
        sq = kx * kx
        s_lo = jnp.sum(jnp.where(lo_half, sq, 0.0), axis=-1, keepdims=True)
        s_hi = jnp.sum(jnp.where(lo_half, 0.0, sq), axis=-1, keepdims=True)
        rkk_ref[:, sl] = kx * lax.rsqrt(jnp.where(lo_half, s_lo, s_hi) + L2_EPS)
        rk_ref[:, sl] = kr * (1.0 + (a - 1.0) * ka_ref[:, sl])

    pbuf[0:HALO, :] = pbuf[tm:tm + HALO, :]


def _proj_prep(x1, gmix, win, convw, alog, dtb, mu, w0, a0, k_k, k_a, w2a2, g2, tm=256):
    b, t, _ = x1.shape
    wide = jax.ShapeDtypeStruct((b, t, GDN_W), F32)
    narrow = jax.ShapeDtypeStruct((b, t, V7X_LANES), F32)
    spec_w = pl.BlockSpec((None, tm, GDN_W), lambda i, j: (i, j, 0))
    spec_n = pl.BlockSpec((None, tm, V7X_LANES), lambda i, j: (i, j, 0))
    consts = (gmix, win, convw, alog, dtb, mu, w0, a0, k_k, k_a, w2a2, g2)
    return pl.pallas_call(
        _proj_kernel,
        out_shape=(wide, wide, wide, wide, narrow, wide, wide, wide, wide, wide, wide, wide),
        grid=(b, t // tm),
        in_specs=[pl.BlockSpec((None, tm, D_MODEL), lambda i, j: (i, j, 0))] + [_const_spec(c.shape) for c in consts],
        out_specs=(spec_w, spec_w, spec_w, spec_w, spec_n, spec_w, spec_w, spec_w, spec_w, spec_w, spec_w, spec_w),
        scratch_shapes=[pltpu.VMEM((tm + HALO, N_PROJ), F32)],
        compiler_params=pltpu.CompilerParams(dimension_semantics=("arbitrary", "arbitrary"),
                                             vmem_limit_bytes=VMEM_LIMIT),
        name="proj_prep",
    )(x1, *consts)


def _gdn_kernel(q_ref, k_ref, v_ref, z_ref, bg_ref, nw_ref, y_ref,
                s_ref, zl_scr, u_scr, aqk_scr, kdt_scr, egl_scr):
    tt = q_ref.shape[0]
    c = GDN_CHUNK
    nh = GDN_HEADS

    @pl.when(pl.program_id(1) == 0)
    def _():
        s_ref[...] = jnp.zeros(s_ref.shape, F32)

    row = lax.broadcasted_iota(jnp.int32, (c, c), 0)
    col = lax.broadcasted_iota(jnp.int32, (c, c), 1)
    incl = row >= col
    strict = row > col
    tri = jnp.where(incl, 1.0, 0.0).astype(BF16)

    def prepare(gi, carry):
        units = []
        for cc in range(GDN_GROUP):
            ci = gi * GDN_GROUP + cc
            rows = pl.ds(pl.multiple_of(ci * c, c), c)
            bg = bg_ref[rows, :]
            gc = _cumsum_rows(tri, bg)
            gct = gc.T
            for hd in range(nh):
                sl = slice(hd * GDN_D, (hd + 1) * GDN_D)
                gcol = gc[:, nh + hd:nh + hd + 1]
                grow = gct[nh + hd:nh + hd + 1, :]
                glast = gc[c - 1:c, nh + hd:nh + hd + 1]
                beta = bg[:, hd:hd + 1]
                k = k_ref[rows, sl]
                units.append(dict(
                    ci=ci, hd=hd, gcol=gcol, glast=glast, k=k, q=q_ref[rows, sl], kb=k * beta,
                    vb=v_ref[rows, sl] * beta, eg=jnp.exp(gcol),
                    dec=jnp.where(incl, jnp.exp(jnp.where(incl, gcol - grow, 0.0)), 0.0)))
        kqs = [_dot_nt(jnp.concatenate([un["kb"], un["q"]], axis=0), un["k"]) for un in units]
        tinvs = _unit_lower_inverses([jnp.where(strict, kq[:c] * un["dec"], 0.0) for kq, un in zip(kqs, units)], c)
        uws = [_dot(tinv, jnp.concatenate([un["vb"], un["kb"] * un["eg"]], axis=1))
               for tinv, un in zip(tinvs, units)]
        for un, kq, uw in zip(units, kqs, uws):
            ci, hd = un["ci"], un["hd"]
            zl_scr[ci, hd] = jnp.concatenate([uw[:, GDN_D:], un["q"] * un["eg"]], axis=0).astype(BF16)
            u_scr[ci, hd] = uw[:, :GDN_D]
            aqk_scr[ci, hd] = (kq[c:] * un["dec"]).astype(BF16)
            kdt_scr[ci, hd] = (un["k"] * jnp.exp(un["glast"] - un["gcol"])).T.astype(BF16)
            egl_scr[ci, hd] = jnp.broadcast_to(jnp.exp(un["glast"]), (8, GDN_D))
        return carry

    lax.fori_loop(0, tt // (c * GDN_GROUP), prepare, 0)

    def scan(ci, carry):
        rows = pl.ds(pl.multiple_of(ci * c, c), c)
        zss = [_dot(zl_scr[ci, hd], s_ref[hd]) for hd in range(nh)]
        v_news = [u_scr[ci, hd] - zss[hd][:c] for hd in range(nh)]
        outs = [zss[hd][c:] + _dot(aqk_scr[ci, hd], v_news[hd]) for hd in range(nh)]
        upds = [_dot(kdt_scr[ci, hd], v_news[hd]) for hd in range(nh)]
        for hd in range(nh):
            s_ref[hd] = s_ref[hd] * egl_scr[ci, hd][0:1, :] + upds[hd]
        for hd in range(nh):
            sl = slice(hd * GDN_D, (hd + 1) * GDN_D)
            o = outs[hd]
            o = o * lax.rsqrt(jnp.mean(o * o, axis=-1, keepdims=True) + NORM_EPS) * nw_ref[...]
            y_ref[rows, sl] = o * _silu(z_ref[rows, sl])
        return carry

    lax.fori_loop(0, tt // c, scan, 0)


def _gdn(q, k, v, z, bg, nw, tt=512):
    b, t, _ = q.shape
    nc = tt // GDN_CHUNK
    spec_w = pl.BlockSpec((None, tt, GDN_W), lambda i, j: (i, j, 0))
    spec_n = pl.BlockSpec((None, tt, V7X_LANES), lambda i, j: (i, j, 0))
    return pl.pallas_call(
        _gdn_kernel,
        out_shape=jax.ShapeDtypeStruct((b, t, GDN_W), F32),
        grid=(b, t // tt),
        in_specs=[spec_w, spec_w, spec_w, spec_w, spec_n, _const_spec((1, GDN_D))],
        out_specs=spec_w,
        scratch_shapes=[pltpu.VMEM((GDN_HEADS, GDN_D, GDN_D), F32),
                        pltpu.VMEM((nc, GDN_HEADS, 2 * GDN_CHUNK, GDN_D), BF16),
                        pltpu.VMEM((nc, GDN_HEADS, GDN_CHUNK, GDN_D), F32),
                        pltpu.VMEM((nc, GDN_HEADS, GDN_CHUNK, GDN_CHUNK), BF16),
                        pltpu.VMEM((nc, GDN_HEADS, GDN_D, GDN_CHUNK), BF16),
                        pltpu.VMEM((nc, GDN_HEADS, 8, GDN_D), F32)],
        compiler_params=pltpu.CompilerParams(dimension_semantics=("arbitrary", "arbitrary"),
                                             vmem_limit_bytes=VMEM_LIMIT),
        name="gdn",
    )(q, k, v, z, bg, nw)


def _rwkv_kernel(r_ref, lw_ref, k_ref, v_ref, kk_ref, a_ref, gate_ref, rk_ref, gnw_ref, gnb_ref, y_ref,
                 s_ref, zl_scr, ut_scr, yv_scr, arb_scr, kbt_scr, vb_scr, dm_scr):
    tt = r_ref.shape[0]
    c = RWKV_CHUNK
    n = RWKV_N
    npair = RWKV_HEADS // 2

    @pl.when(pl.program_id(1) == 0)
    def _():
        s_ref[...] = jnp.zeros(s_ref.shape, F32)

    row_c = lax.broadcasted_iota(jnp.int32, (c, c), 0)
    col_c = lax.broadcasted_iota(jnp.int32, (c, c), 1)
    tri = jnp.where(row_c >= col_c, 1.0, 0.0).astype(BF16)
    row = lax.broadcasted_iota(jnp.int32, (c, 2 * n), 0)
    lane = lax.broadcasted_iota(jnp.int32, (c, 2 * n), 1)
    m0 = lane < n
    jj = jnp.where(m0, lane, lane - n)
    strict = row > jj
    incl = row >= jj
    row2 = lax.broadcasted_iota(jnp.int32, (2 * n, 2 * n), 0)
    lane2 = lax.broadcasted_iota(jnp.int32, (2 * n, 2 * n), 1)
    same_head = (row2 < n) == (lane2 < n)

    def sel0(x):
        return jnp.where(m0, x, 0.0)

    def sel1(x):
        return jnp.where(m0, 0.0, x)

    def head_sum(t):
        s_lo = jnp.sum(sel0(t), axis=-1, keepdims=True)
        s_hi = jnp.sum(sel1(t), axis=-1, keepdims=True)
        return jnp.where(m0, s_lo, s_hi)

    def prepare(gi, carry):
        units = []
        for cc in range(RWKV_GROUP):
            ci = gi * RWKV_GROUP + cc
            rows = pl.ds(pl.multiple_of(ci * c, c), c)
            g_all = _cumsum_rows(tri, lw_ref[rows, :])
            for p in range(npair):
                sl = slice(p * 2 * n, (p + 1) * 2 * n)
                g = g_all[:, sl]
                gmid = g[c // 2 - 1:c // 2, :]
                gend = g[c - 1:c, :]
                e_inv = jnp.exp(gmid - g)
                e_end = jnp.exp(gend - g)
                egm = jnp.exp(gmid)
                k = k_ref[rows, sl]
                kk = kk_ref[rows, sl]
                b = kk * a_ref[rows, sl]
                kap_t = kk * jnp.exp(g - lw_ref[rows, sl] - gmid)
                r_t = r_ref[rows, sl] * jnp.exp(g - gmid)
                units.append(dict(ci=ci, p=p, v=v_ref[rows, sl], kap_t=kap_t, r_t=r_t, b_t=b * e_inv, k_t=k * e_inv,
                                  kb=jnp.concatenate([k * e_end, -(b * e_end)], axis=0), egm=egm,
                                  egend=jnp.exp(gend)))
        a0s = [_dot_nt(jnp.concatenate([sel0(un["kap_t"]), sel0(un["r_t"])], axis=0),
                       jnp.concatenate([un["b_t"], un["k_t"]], axis=0)) for un in units]
        a1s = [_dot_nt(jnp.concatenate([sel1(un["kap_t"]), sel1(un["r_t"])], axis=0),
                       jnp.concatenate([un["k_t"], un["b_t"]], axis=0)) for un in units]
        lmats = [jnp.concatenate([jnp.where(strict & m0, a0[:c], 0.0), jnp.where(strict & (~m0), a1[:c], 0.0)], axis=0)
                 for a0, a1 in zip(a0s, a1s)]
        avys = []
        for un, a0, a1 in zip(units, a0s, a1s):
            a_uk = jnp.where(strict, jnp.where(m0, a1[:c], a0[:c]), 0.0)
            a_rk = jnp.where(incl, jnp.where(m0, a1[c:], a0[c:]), 0.0)
            v_sw = jnp.concatenate([sel1(un["v"]), sel0(un["v"])], axis=0)
            avys.append(_dot(jnp.concatenate([a_uk, a_rk], axis=0), v_sw))
        tbds = _unit_lower_inverses(lmats, c)
        wus = []
        for un, tbd, avy in zip(units, tbds, avys):
            tcat = tbd[:c] + tbd[c:]
            kap_b = un["kap_t"] * un["egm"]
            av = avy[:c]
            x = jnp.concatenate([jnp.concatenate([sel0(kap_b), sel0(av)], axis=1),
                                 jnp.concatenate([sel1(kap_b), sel1(av)], axis=1)], axis=0)
            wus.append(_dot(tcat, x))
        for un, a0, a1, avy, wu in zip(units, a0s, a1s, avys, wus):
            ci, p = un["ci"], un["p"]
            zl_scr[ci, p] = jnp.concatenate([wu[:, :2 * n], un["r_t"] * un["egm"]], axis=0).astype(BF16)
            ut_scr[ci, p] = wu[:, 2 * n:]
            yv_scr[ci, p] = avy[c:]
            arb_scr[ci, p] = jnp.where(incl, jnp.where(m0, a0[c:], a1[c:]), 0.0).astype(BF16)
            kbt_scr[ci, p] = un["kb"].T.astype(BF16)
            vb_scr[ci, p] = un["v"].astype(BF16)
            dm_scr[ci, p] = jnp.broadcast_to(un["egend"], (2 * n, 2 * n)).T
        return carry

    lax.fori_loop(0, tt // (c * RWKV_GROUP), prepare, 0)

    def scan(ci, carry):
        rows = pl.ds(pl.multiple_of(ci * c, c), c)
        zss = [_dot(zl_scr[ci, p], s_ref[p]) for p in range(npair)]
        us = [zss[p][:c] + ut_scr[ci, p] for p in range(npair)]
        ys = [zss[p][c:] - _dot(arb_scr[ci, p], jnp.concatenate([sel0(us[p]), sel1(us[p])], axis=0)) + yv_scr[ci, p]
              for p in range(npair)]
        upds = [jnp.dot(kbt_scr[ci, p], jnp.concatenate([vb_scr[ci, p], us[p].astype(BF16)], axis=0),
                        preferred_element_type=F32) for p in range(npair)]
        for p in range(npair):
            s_ref[p] = s_ref[p] * dm_scr[ci, p] + jnp.where(same_head, upds[p], 0.0)
        for p in range(npair):
            sl = slice(p * 2 * n, (p + 1) * 2 * n)
            y = ys[p]
            mean = head_sum(y) * (1.0 / n)
            dlt = y - mean
            var = head_sum(dlt * dlt) * (1.0 / n)
            yn = dlt * lax.rsqrt(var + GN_EPS) * gnw_ref[:, sl] + gnb_ref[:, sl]
            v = v_ref[rows, sl]
            bonus = head_sum(r_ref[rows, sl] * k_ref[rows, sl] * rk_ref[:, sl]) * v
            y_ref[rows, sl] = (yn + bonus) * gate_ref[rows, sl]
        return carry

    lax.fori_loop(0, tt // c, scan, 0)


def _rwkv(r, lw, k, v, kk, a, gate, r_k, gn_w, gn_b, tt=256):
    b, t, _ = r.shape
    nc = tt // RWKV_CHUNK
    npair = RWKV_HEADS // 2
    slab = 2 * RWKV_N
    spec_w = pl.BlockSpec((None, tt, RWKV_W), lambda i, j: (i, j, 0))
    cs = _const_spec((1, RWKV_W))
    return pl.pallas_call(
        _rwkv_kernel,
        out_shape=jax.ShapeDtypeStruct((b, t, RWKV_W), F32),
        grid=(b, t // tt),
        in_specs=[spec_w] * 7 + [cs, cs, cs],
        out_specs=spec_w,
        scratch_shapes=[pltpu.VMEM((npair, slab, slab), F32),
                        pltpu.VMEM((nc, npair, 2 * RWKV_CHUNK, slab), BF16),
                        pltpu.VMEM((nc, npair, RWKV_CHUNK, slab), F32),
                        pltpu.VMEM((nc, npair, RWKV_CHUNK, slab), F32),
                        pltpu.VMEM((nc, npair, RWKV_CHUNK, slab), BF16),
                        pltpu.VMEM((nc, npair, slab, 2 * RWKV_CHUNK), BF16),
                        pltpu.VMEM((nc, npair, RWKV_CHUNK, slab), BF16),
                        pltpu.VMEM((nc, npair, slab, slab), F32)],
        compiler_params=pltpu.CompilerParams(dimension_semantics=("arbitrary", "arbitrary"),
                                             vmem_limit_bytes=VMEM_LIMIT),
        name="rwkv",
    )(r, lw, k, v, kk, a, gate, r_k, gn_w, gn_b)


def _tail_kernel(x_ref, yg_ref, yr_ref, p_ref, wout_ref, gmp_ref, gpre_ref, wg_ref, wu_ref, wd_ref, gpost_ref,
                 gple_ref, wpg_ref, wpp_ref, gplep_ref, o_ref, a_scr):
    m = (jnp.dot(yg_ref[...].astype(BF16), wout_ref[0:GDN_W, :], preferred_element_type=F32)
         + jnp.dot(yr_ref[...].astype(BF16), wout_ref[GDN_W:GDN_W + RWKV_W, :], preferred_element_type=F32))
    x = x_ref[...] + _rms(m, gmp_ref[...])
    x = _swiglu_ffn(x, gpre_ref[...], wg_ref, wu_ref, wd_ref, gpost_ref[...], a_scr)
    gate = _sigmoid(jnp.dot(_rms(x, gple_ref[...]).astype(BF16), wpg_ref[...], preferred_element_type=F32))
    emb = jnp.dot(p_ref[...].astype(BF16), wpp_ref[...], preferred_element_type=F32)
    o_ref[...] = x + _rms(gate * emb, gplep_ref[...])


def _tail(x1, yg, yr, p, wout, gmp, gpre, wg, wu, wd, gpost, gple, wpg, wpp, gplep, tm=256):
    n = x1.shape[0]
    row = pl.BlockSpec((tm, D_MODEL), lambda i: (i, 0))
    half = pl.BlockSpec((tm, GDN_W), lambda i: (i, 0))
    ple = pl.BlockSpec((tm, D_PLE), lambda i: (i, 0))
    consts = (wout, gmp, gpre, wg, wu, wd, gpost, gple, wpg, wpp, gplep)
    return pl.pallas_call(
        _tail_kernel,
        out_shape=jax.ShapeDtypeStruct(x1.shape, F32),
        grid=(n // tm,),
        in_specs=[row, half, half, ple] + [_const_spec(c.shape) for c in consts],
        out_specs=row,
        scratch_shapes=[pltpu.VMEM((tm, D_FF), BF16)],
        compiler_params=pltpu.CompilerParams(dimension_semantics=("arbitrary",), vmem_limit_bytes=VMEM_LIMIT),
        name="tail",
    )(x1, yg, yr, p, *consts)


def _row(v):
    return v.reshape(1, -1).astype(F32)


def _layer(x, p, ffn1_norm_pre, ffn1_w_gate, ffn1_w_up, ffn1_w_down, ffn1_norm_post, mix_norm_pre, w_in,
           gdn_conv_w, gdn_a_log, gdn_dt_bias, gdn_norm_w, rwkv_mu, rwkv_w0, rwkv_w2, rwkv_a0, rwkv_a2, rwkv_g2,
           rwkv_k_k, rwkv_k_a, rwkv_r_k, rwkv_gn_w, rwkv_gn_b, w_out, mix_norm_post, ffn2_norm_pre, ffn2_w_gate,
           ffn2_w_up, ffn2_w_down, ffn2_norm_post, ple_norm_pre, ple_w_gate, ple_w_proj, ple_norm_post):
    b, t, d = x.shape
    n = b * t
    x1 = _ffn(x.reshape(n, d), _row(ffn1_norm_pre), ffn1_w_gate.astype(BF16), ffn1_w_up.astype(BF16),
              ffn1_w_down.astype(BF16), _row(ffn1_norm_post))

    n_gdn_main = 4 * GDN_W
    n_bg = 2 * GDN_HEADS
    win = jnp.concatenate([w_in[:, :n_gdn_main], w_in[:, n_gdn_main + n_bg:], w_in[:, n_gdn_main:n_gdn_main + n_bg],
                           jnp.zeros((d, V7X_LANES - n_bg), w_in.dtype)], axis=1).astype(BF16)
    pad_bg = lambda vec: jnp.concatenate([jnp.zeros((GDN_HEADS,), F32), vec.astype(F32),
                                          jnp.zeros((V7X_LANES - n_bg,), F32)]).reshape(1, V7X_LANES)
    zero_l = jnp.zeros((RWKV_N, RWKV_W), F32)
    w2a2 = jnp.concatenate([jnp.concatenate([rwkv_w2.astype(F32), zero_l], axis=1),
                            jnp.concatenate([zero_l, rwkv_a2.astype(F32)], axis=1)], axis=0).astype(BF16)
    outs = _proj_prep(x1.reshape(b, t, d), _row(mix_norm_pre), win, gdn_conv_w.astype(F32), pad_bg(gdn_a_log),
                      pad_bg(gdn_dt_bias), _row(rwkv_mu), _row(rwkv_w0), _row(rwkv_a0), _row(rwkv_k_k),
                      _row(rwkv_k_a), w2a2, rwkv_g2.astype(BF16))
    gq, gk, gv, gz, gbg, rr, rlw, rk, rv, rkk, ra, rgate = outs
    y_gdn = _gdn(gq, gk, gv, gz, gbg, _row(gdn_norm_w))
    y_rwkv = _rwkv(rr, rlw, rk, rv, rkk, ra, rgate, _row(rwkv_r_k), _row(rwkv_gn_w), _row(rwkv_gn_b))
    out = _tail(x1, y_gdn.reshape(n, GDN_W), y_rwkv.reshape(n, RWKV_W), p.reshape(n, D_PLE), w_out.astype(BF16),
                _row(mix_norm_post), _row(ffn2_norm_pre), ffn2_w_gate.astype(BF16), ffn2_w_up.astype(BF16),
                ffn2_w_down.astype(BF16), _row(ffn2_norm_post), _row(ple_norm_pre), ple_w_gate.astype(BF16),
                ple_w_proj.astype(BF16), _row(ple_norm_post))
    return out.reshape(b, t, d)


def kernel(x, p, ffn1_norm_pre, ffn1_w_gate, ffn1_w_up, ffn1_w_down, ffn1_norm_post, mix_norm_pre, w_in, gdn_conv_w, gdn_a_log, gdn_dt_bias, gdn_norm_w, rwkv_mu, rwkv_w0, rwkv_w2, rwkv_a0, rwkv_a2, rwkv_g2, rwkv_k_k, rwkv_k_a, rwkv_r_k, rwkv_gn_w, rwkv_gn_b, w_out, mix_norm_post, ffn2_norm_pre, ffn2_w_gate, ffn2_w_up, ffn2_w_down, ffn2_norm_post, ple_norm_pre, ple_w_gate, ple_w_proj, ple_norm_post):
    return _layer(x, p[0], ffn1_norm_pre[0], ffn1_w_gate[0], ffn1_w_up[0], ffn1_w_down[0], ffn1_norm_post[0],
                  mix_norm_pre[0], w_in[0], gdn_conv_w[0], gdn_a_log[0], gdn_dt_bias[0], gdn_norm_w[0], rwkv_mu[0],
                  rwkv_w0[0], rwkv_w2[0], rwkv_a0[0], rwkv_a2[0], rwkv_g2[0], rwkv_k_k[0], rwkv_k_a[0], rwkv_r_k[0],
                  rwkv_gn_w[0], rwkv_gn_b[0], w_out[0], mix_norm_post[0], ffn2_norm_pre[0], ffn2_w_gate[0],
                  ffn2_w_up[0], ffn2_w_down[0], ffn2_norm_post[0], ple_norm_pre[0], ple_w_gate[0], ple_w_proj[0],
                  ple_norm_post[0])
```

```python
import jax
import jax.numpy as jnp
from jax import lax
from jax.experimental import pallas as pl
from jax.experimental.pallas import tpu as pltpu

F32 = jnp.float32
BF16 = jnp.bfloat16

D_MODEL = 1024
D_FF = 2816
D_PLE = 256
GDN_HEADS = 4
GDN_D = 128
GDN_CONV = 4
RWKV_HEADS = 8
RWKV_N = 64
RWKV_W = RWKV_HEADS * RWKV_N
GDN_W = GDN_HEADS * GDN_D
LORA_WA = 128
LORA_G = 128
NORM_EPS = 1e-6
GN_EPS = 64e-5
L2_EPS = 1e-6

V7X_LANES = 128

GCOL_Z = 3 * GDN_W
GCOL_BG = 4 * GDN_W
GDN_COLS = GCOL_BG + V7X_LANES
RW_COLS = 3 * RWKV_W + LORA_WA + LORA_G

GDN_CHUNK = 128
RWKV_CHUNK = 64
HALO = 8
GDN_TILE = 512
RWKV_TILE = 256

FF_CHUNKS = ((0, 768), (768, 768), (1536, 768), (2304, 512))
VMEM_LIMIT = 56 * 1024 * 1024


def _dot(a, b):
    return jnp.dot(a.astype(BF16), b.astype(BF16), preferred_element_type=F32)


def _dot_nt(a, b):
    return lax.dot_general(a.astype(BF16), b.astype(BF16), (((1,), (1,)), ((), ())),
                           preferred_element_type=F32)


def _rms(x, g):
    return x * lax.rsqrt(jnp.mean(x * x, axis=-1, keepdims=True) + NORM_EPS) * g


def _sigmoid(x):
    return 1.0 / (1.0 + jnp.exp(-x))


def _silu(x):
    h = 0.5 * x
    return h + h * jnp.tanh(h)


def _softplus(x):
    return jnp.maximum(x, 0.0) + jnp.log(1.0 + jnp.exp(-jnp.abs(x)))


def _split3(x):
    hi = x.astype(BF16)
    r1 = x - hi.astype(F32)
    mid = r1.astype(BF16)
    lo = (r1 - mid.astype(F32)).astype(BF16)
    return hi, mid, lo


def _cumsum_rows(tri, x):
    hi, mid, lo = _split3(x)
    return (jnp.dot(tri, hi, preferred_element_type=F32)
            + jnp.dot(tri, mid, preferred_element_type=F32)
            + jnp.dot(tri, lo, preferred_element_type=F32))


def _odd_blocks(x, b):
    return jnp.concatenate([x[s:s + b] for s in range(b, x.shape[0], 2 * b)], axis=0)


def _with_odd_blocks(base, odd, b):
    parts = []
    for j, s in enumerate(range(0, base.shape[0], 2 * b)):
        parts.append(base[s:s + b])
        parts.append(odd[j * b:(j + 1) * b])
    return jnp.concatenate(parts, axis=0)


def _unit_lower_inverses(lmats, top):
    n = lmats[0].shape[0]
    row = lax.broadcasted_iota(jnp.int32, (n, n), 0)
    col = lax.broadcasted_iota(jnp.int32, (n, n), 1)
    lower = row > col
    eye = jnp.where(row == col, 1.0, 0.0)
    xs = [eye - jnp.where(lower & ((row // 2) == (col // 2)), lm, 0.0) for lm in lmats]
    b = 2
    while b < top:
        sub = lower & ((row // (2 * b)) == (col // (2 * b))) & ((row // b) != (col // b))
        cs = [jnp.where(sub, lm, 0.0) for lm in lmats]
        if b % 8:
            ys = [_dot(c, x) for c, x in zip(cs, xs)]
            xs = [x - _dot(x, y) for x, y in zip(xs, ys)]
        else:
            zero = jnp.zeros((n, n), F32)
            ys = [_with_odd_blocks(zero, _dot(_odd_blocks(c, b), x), b) for c, x in zip(cs, xs)]
            xs = [_with_odd_blocks(x, _odd_blocks(x, b) - _dot(_odd_blocks(x, b), y), b) for x, y in zip(xs, ys)]
        b *= 2
    return xs


def _swiglu_ffn(x, gpre, wg_ref, wu_ref, wd_ref, gpost, a_scr):
    h = _rms(x, gpre).astype(BF16)
    for s, w in FF_CHUNKS:
        g = jnp.dot(h, wg_ref[:, s:s + w], preferred_element_type=F32)
        u = jnp.dot(h, wu_ref[:, s:s + w], preferred_element_type=F32)
        a_scr[:, s:s + w] = (_silu(g) * u).astype(BF16)
    f = jnp.dot(a_scr[...], wd_ref[...], preferred_element_type=F32)
    return x + 0.5 * _rms(f, gpost)


def _const_spec(shape):
    return pl.BlockSpec(shape, lambda *_: (0,) * len(shape), pipeline_mode=pl.Buffered(1))


def _project(x_ref, gmix_ref, win_ref, pbuf):
    tile = x_ref.shape[0]

    @pl.when(pl.program_id(1) == 0)
    def _():
        pbuf[0:HALO, :] = jnp.zeros((HALO, pbuf.shape[1]), F32)

    h = _rms(x_ref[...], gmix_ref[...]).astype(BF16)
    pbuf[HALO:HALO + tile, :] = jnp.dot(h, win_ref[...], preferred_element_type=F32)


def _keep_halo(pbuf, tile):
    pbuf[0:HALO, :] = pbuf[tile:tile + HALO, :]


def _ffn_kernel(x_ref, gpre_ref, wg_ref, wu_ref, wd_ref, gpost_ref, o_ref, a_scr):
    o_ref[...] = _swiglu_ffn(x_ref[...], gpre_ref[...], wg_ref, wu_ref, wd_ref, gpost_ref[...], a_scr)


def _ffn(x2d, gpre, wg, wu, wd, gpost, tm=512):
    n = x2d.shape[0]
    row = pl.BlockSpec((tm, D_MODEL), lambda i: (i, 0))
    return pl.pallas_call(
        _ffn_kernel,
        out_shape=jax.ShapeDtypeStruct(x2d.shape, F32),
        grid=(n // tm,),
        in_specs=[row, _const_spec((1, D_MODEL)), _const_spec((D_MODEL, D_FF)), _const_spec((D_MODEL, D_FF)),
                  _const_spec((D_FF, D_MODEL)), _const_spec((1, D_MODEL))],
        out_specs=row,
        scratch_shapes=[pltpu.VMEM((tm, D_FF), BF16)],
        compiler_params=pltpu.CompilerParams(dimension_semantics=("arbitrary",), vmem_limit_bytes=VMEM_LIMIT),
        name="ffn",
    )(x2d, gpre, wg, wu, wd, gpost)


def _gdn_kernel(x_ref, gmix_ref, win_ref, convw_ref, alog_ref, dtb_ref, nw_ref, y_ref,
                pbuf, q_s, k_s, v_s, bg_s, s_ref, zl_scr, u_scr, aqk_scr, kdt_scr, egl_scr):
    tt = x_ref.shape[0]
    c = GDN_CHUNK
    nh = GDN_HEADS

    @pl.when(pl.program_id(1) == 0)
    def _():
        s_ref[...] = jnp.zeros(s_ref.shape, F32)

    _project(x_ref, gmix_ref, win_ref, pbuf)

    for grp, dst in ((0, q_s), (1, k_s), (2, v_s)):
        for hd in range(nh):
            c0 = grp * GDN_W + hd * GDN_D
            xe = pbuf[0:HALO + tt, c0:c0 + GDN_D]
            acc = xe * convw_ref[0:1, c0:c0 + GDN_D]
            for j in range(1, GDN_CONV):
                acc = pltpu.roll(acc, 1, axis=0) + xe * convw_ref[j:j + 1, c0:c0 + GDN_D]
            y = _silu(acc[HALO:])
            if grp < 2:
                y = y * lax.rsqrt(jnp.sum(y * y, axis=-1, keepdims=True) + L2_EPS)
            if grp == 0:
                y = y * (GDN_D ** -0.5)
            dst[:, hd * GDN_D:(hd + 1) * GDN_D] = y
    lane = lax.broadcasted_iota(jnp.int32, (tt, V7X_LANES), 1)
    pb = pbuf[HALO:HALO + tt, GCOL_BG:GCOL_BG + V7X_LANES]
    gdec = -jnp.exp(alog_ref[...]) * _softplus(pb + dtb_ref[...])
    bg_s[...] = jnp.where(lane < nh, _sigmoid(pb), jnp.where(lane < 2 * nh, gdec, 0.0))

    row = lax.broadcasted_iota(jnp.int32, (c, c), 0)
    col = lax.broadcasted_iota(jnp.int32, (c, c), 1)
    incl = row >= col
    strict = row > col
    tri = jnp.where(incl, 1.0, 0.0).astype(BF16)

    units = []
    for ci in range(tt // c):
        rows = slice(ci * c, (ci + 1) * c)
        bg = bg_s[rows, :]
        gc = _cumsum_rows(tri, bg)
        gct = gc.T
        for hd in range(nh):
            sl = slice(hd * GDN_D, (hd + 1) * GDN_D)
            gcol = gc[:, nh + hd:nh + hd + 1]
            grow = gct[nh + hd:nh + hd + 1, :]
            glast = gc[c - 1:c, nh + hd:nh + hd + 1]
            beta = bg[:, hd:hd + 1]
            k = k_s[rows, sl]
            units.append(dict(
                ci=ci, hd=hd, gcol=gcol, glast=glast, k=k, q=q_s[rows, sl], kb=k * beta,
                vb=v_s[rows, sl] * beta, eg=jnp.exp(gcol),
                dec=jnp.where(incl, jnp.exp(jnp.where(incl, gcol - grow, 0.0)), 0.0)))
    kqs = [_dot_nt(jnp.concatenate([un["kb"], un["q"]], axis=0), un["k"]) for un in units]
    tinvs = _unit_lower_inverses([jnp.where(strict, kq[:c] * un["dec"], 0.0) for kq, un in zip(kqs, units)], c)
    uws = [_dot(tinv, jnp.concatenate([un["vb"], un["kb"] * un["eg"]], axis=1))
           for tinv, un in zip(tinvs, units)]
    for un, kq, uw in zip(units, kqs, uws):
        ci, hd = un["ci"], un["hd"]
        zl_scr[ci, hd] = jnp.concatenate([uw[:, GDN_D:], un["q"] * un["eg"]], axis=0).astype(BF16)
        u_scr[ci, hd] = uw[:, :GDN_D]
        aqk_scr[ci, hd] = (kq[c:] * un["dec"]).astype(BF16)
        kdt_scr[ci, hd] = (un["k"] * jnp.exp(un["glast"] - un["gcol"])).T.astype(BF16)
        egl_scr[ci, hd] = jnp.broadcast_to(jnp.exp(un["glast"]), (8, GDN_D))

    def scan(ci, carry):
        rows = pl.ds(pl.multiple_of(ci * c, c), c)
        zrows = pl.ds(pl.multiple_of(ci * c + HALO, HALO), c)
        zss = [_dot(zl_scr[ci, hd], s_ref[hd]) for hd in range(nh)]
        v_news = [u_scr[ci, hd] - zss[hd][:c] for hd in range(nh)]
        outs = [zss[hd][c:] + _dot(aqk_scr[ci, hd], v_news[hd]) for hd in range(nh)]
        upds = [_dot(kdt_scr[ci, hd], v_news[hd]) for hd in range(nh)]
        for hd in range(nh):
            s_ref[hd] = s_ref[hd] * egl_scr[ci, hd][0:1, :] + upds[hd]
        for hd in range(nh):
            o = outs[hd]
            o = o * lax.rsqrt(jnp.mean(o * o, axis=-1, keepdims=True) + NORM_EPS) * nw_ref[...]
            z = pbuf[zrows, GCOL_Z + hd * GDN_D:GCOL_Z + (hd + 1) * GDN_D]
            y_ref[rows, hd * GDN_D:(hd + 1) * GDN_D] = (o * _silu(z)).astype(y_ref.dtype)
        return carry

    lax.fori_loop(0, tt // c, scan, 0)
    _keep_halo(pbuf, tt)


def _gdn(x1, gmix, win, convw, alog, dtb, nw):
    b, t, _ = x1.shape
    tt = GDN_TILE
    nc = tt // GDN_CHUNK
    consts = (gmix, win, convw, alog, dtb, nw)
    return pl.pallas_call(
        _gdn_kernel,
        out_shape=jax.ShapeDtypeStruct((b, t, GDN_W), BF16),
        grid=(b, t // tt),
        in_specs=[pl.BlockSpec((None, tt, D_MODEL), lambda i, j: (i, j, 0))] + [_const_spec(c.shape) for c in consts],
        out_specs=pl.BlockSpec((None, tt, GDN_W), lambda i, j: (i, j, 0)),
        scratch_shapes=[pltpu.VMEM((tt + HALO, GDN_COLS), F32),
                        pltpu.VMEM((tt, GDN_W), F32), pltpu.VMEM((tt, GDN_W), F32), pltpu.VMEM((tt, GDN_W), F32),
                        pltpu.VMEM((tt, V7X_LANES), F32),
                        pltpu.VMEM((GDN_HEADS, GDN_D, GDN_D), F32),
                        pltpu.VMEM((nc, GDN_HEADS, 2 * GDN_CHUNK, GDN_D), BF16),
                        pltpu.VMEM((nc, GDN_HEADS, GDN_CHUNK, GDN_D), F32),
                        pltpu.VMEM((nc, GDN_HEADS, GDN_CHUNK, GDN_CHUNK), BF16),
                        pltpu.VMEM((nc, GDN_HEADS, GDN_D, GDN_CHUNK), BF16),
                        pltpu.VMEM((nc, GDN_HEADS, 8, GDN_D), F32)],
        compiler_params=pltpu.CompilerParams(dimension_semantics=("arbitrary", "arbitrary"),
                                             vmem_limit_bytes=VMEM_LIMIT),
        name="gdn",
    )(x1, *consts)


def _rwkv_kernel(x_ref, gmix_ref, win_ref, mu_ref, w0_ref, a0_ref, kkw_ref, kaw_ref, w2a2_ref, g2_ref,
                 rk_ref, gnw_ref, gnb_ref, y_ref,
                 pbuf, r_s, lw_s, k_s, v_s, kk_s, a_s, gate_s,
                 s_ref, zl_scr, ut_scr, yv_scr, arb_scr, kbt_scr, vb_scr, dm_scr):
    tt = x_ref.shape[0]
    c = RWKV_CHUNK
    n = RWKV_N
    npair = RWKV_HEADS // 2

    @pl.when(pl.program_id(1) == 0)
    def _():
        s_ref[...] = jnp.zeros(s_ref.shape, F32)

    _project(x_ref, gmix_ref, win_ref, pbuf)

    lane_t = lax.broadcasted_iota(jnp.int32, (tt, V7X_LANES), 1)
    lo_half = lane_t < n

    def shifted(c0, width):
        cur = pbuf[HALO:HALO + tt, c0:c0 + width]
        prev = pbuf[HALO - 1:HALO - 1 + tt, c0:c0 + width]
        return cur + (prev - cur) * mu_ref[:, c0:c0 + width]

    lwa = shifted(3 * RWKV_W, LORA_WA)
    lwa = jnp.where(lo_half, jnp.tanh(lwa), lwa)
    wa = jnp.dot(lwa.astype(BF16), w2a2_ref[...], preferred_element_type=F32)
    lg = shifted(3 * RWKV_W + LORA_WA, LORA_G)
    gate_s[...] = jnp.dot(_sigmoid(lg).astype(BF16), g2_ref[...], preferred_element_type=F32)
    for p in range(npair):
        sl = slice(p * V7X_LANES, (p + 1) * V7X_LANES)
        w_log = -_softplus(-(w0_ref[:, sl] + wa[:, sl])) - 0.5
        lw_s[:, sl] = -jnp.exp(w_log)
        a = _sigmoid(a0_ref[:, sl] + wa[:, RWKV_W + p * V7X_LANES:RWKV_W + (p + 1) * V7X_LANES])
        a_s[:, sl] = a
        r_s[:, sl] = shifted(p * V7X_LANES, V7X_LANES)
        kr = shifted(RWKV_W + p * V7X_LANES, V7X_LANES)
        v_s[:, sl] = shifted(2 * RWKV_W + p * V7X_LANES, V7X_LANES)
        kx = kr * kkw_ref[:, sl]
        sq = kx * kx
        s_lo = jnp.sum(jnp.where(lo_half, sq, 0.0), axis=-1, keepdims=True)
        s_hi = jnp.sum(jnp.where(lo_half, 0.0, sq), axis=-1, keepdims=True)
        kk_s[:, sl] = kx * lax.rsqrt(jnp.where(lo_half, s_lo, s_hi) + L2_EPS)
        k_s[:, sl] = kr * (1.0 + (a - 1.0) * kaw_ref[:, sl])

    row_c = lax.broadcasted_iota(jnp.int32, (c, c), 0)
    col_c = lax.broadcasted_iota(jnp.int32, (c, c), 1)
    tri = jnp.where(row_c >= col_c, 1.0, 0.0).astype(BF16)
    row = lax.broadcasted_iota(jnp.int32, (c, 2 * n), 0)
    lane = lax.broadcasted_iota(jnp.int32, (c, 2 * n), 1)
    m0 = lane < n
    jj = jnp.where(m0, lane, lane - n)
    strict = row > jj
    incl = row >= jj
    row2 = lax.broadcasted_iota(jnp.int32, (2 * n, 2 * n), 0)
    lane2 = lax.broadcasted_iota(jnp.int32, (2 * n, 2 * n), 1)
    same_head = (row2 < n) == (lane2 < n)

    def sel0(x):
        return jnp.where(m0, x, 0.0)

    def sel1(x):
        return jnp.where(m0, 0.0, x)

    def head_sum(t):
        s_lo = jnp.sum(sel0(t), axis=-1, keepdims=True)
        s_hi = jnp.sum(sel1(t), axis=-1, keepdims=True)
        return jnp.where(m0, s_lo, s_hi)

    units = []
    for ci in range(tt // c):
        rows = slice(ci * c, (ci + 1) * c)
        g_all = _cumsum_rows(tri, lw_s[rows, :])
        for p in range(npair):
            sl = slice(p * 2 * n, (p + 1) * 2 * n)
            g = g_all[:, sl]
            gmid = g[c // 2 - 1:c // 2, :]
            gend = g[c - 1:c, :]
            e_inv = jnp.exp(gmid - g)
            e_end = jnp.exp(gend - g)
            egm = jnp.exp(gmid)
            k = k_s[rows, sl]
            kk = kk_s[rows, sl]
            b = kk * a_s[rows, sl]
            kap_t = kk * jnp.exp(g - lw_s[rows, sl] - gmid)
            r_t = r_s[rows, sl] * jnp.exp(g - gmid)
            units.append(dict(ci=ci, p=p, v=v_s[rows, sl], kap_t=kap_t, r_t=r_t, b_t=b * e_inv, k_t=k * e_inv,
                              kb=jnp.concatenate([k * e_end, -(b * e_end)], axis=0), egm=egm,
                              egend=jnp.exp(gend)))
    a0s = [_dot_nt(jnp.concatenate([sel0(un["kap_t"]), sel0(un["r_t"])], axis=0),
                   jnp.concatenate([un["b_t"], un["k_t"]], axis=0)) for un in units]
    a1s = [_dot_nt(jnp.concatenate([sel1(un["kap_t"]), sel1(un["r_t"])], axis=0),
                   jnp.concatenate([un["k_t"], un["b_t"]], axis=0)) for un in units]
    lmats = [jnp.concatenate([jnp.where(strict & m0, a0[:c], 0.0), jnp.where(strict & (~m0), a1[:c], 0.0)], axis=0)
             for a0, a1 in zip(a0s, a1s)]
    avys = []
    for un, a0, a1 in zip(units, a0s, a1s):
        a_uk = jnp.where(strict, jnp.where(m0, a1[:c], a0[:c]), 0.0)
        a_rk = jnp.where(incl, jnp.where(m0, a1[c:], a0[c:]), 0.0)
        v_sw = jnp.concatenate([sel1(un["v"]), sel0(un["v"])], axis=0)
        avys.append(_dot(jnp.concatenate([a_uk, a_rk], axis=0), v_sw))
    tbds = _unit_lower_inverses(lmats, c)
    wus = []
    for un, tbd, avy in zip(units, tbds, avys):
        tcat = tbd[:c] + tbd[c:]
        kap_b = un["kap_t"] * un["egm"]
        av = avy[:c]
        x = jnp.concatenate([jnp.concatenate([sel0(kap_b), sel0(av)], axis=1),
                             jnp.concatenate([sel1(kap_b), sel1(av)], axis=1)], axis=0)
        wus.append(_dot(tcat, x))
    for un, a0, a1, avy, wu in zip(units, a0s, a1s, avys, wus):
        ci, p = un["ci"], un["p"]
        zl_scr[ci, p] = jnp.concatenate([wu[:, :2 * n], un["r_t"] * un["egm"]], axis=0).astype(BF16)
        ut_scr[ci, p] = wu[:, 2 * n:]
        yv_scr[ci, p] = avy[c:]
        arb_scr[ci, p] = jnp.where(incl, jnp.where(m0, a0[c:], a1[c:]), 0.0).astype(BF16)
        kbt_scr[ci, p] = un["kb"].T.astype(BF16)
        vb_scr[ci, p] = un["v"].astype(BF16)
        dm_scr[ci, p] = jnp.broadcast_to(un["egend"], (2 * n, 2 * n)).T

    def scan(ci, carry):
        rows = pl.ds(pl.multiple_of(ci * c, c), c)
        zss = [_dot(zl_scr[ci, p], s_ref[p]) for p in range(npair)]
        us = [zss[p][:c] + ut_scr[ci, p] for p in range(npair)]
        ys = [zss[p][c:] - _dot(arb_scr[ci, p], jnp.concatenate([sel0(us[p]), sel1(us[p])], axis=0)) + yv_scr[ci, p]
              for p in range(npair)]
        upds = [jnp.dot(kbt_scr[ci, p], jnp.concatenate([vb_scr[ci, p], us[p].astype(BF16)], axis=0),
                        preferred_element_type=F32) for p in range(npair)]
        for p in range(npair):
            s_ref[p] = s_ref[p] * dm_scr[ci, p] + jnp.where(same_head, upds[p], 0.0)
        for p in range(npair):
            sl = slice(p * 2 * n, (p + 1) * 2 * n)
            y = ys[p]
            mean = head_sum(y) * (1.0 / n)
            dlt = y - mean
            var = head_sum(dlt * dlt) * (1.0 / n)
            yn = dlt * lax.rsqrt(var + GN_EPS) * gnw_ref[:, sl] + gnb_ref[:, sl]
            v = v_s[rows, sl]
            bonus = head_sum(r_s[rows, sl] * k_s[rows, sl] * rk_ref[:, sl]) * v
            y_ref[rows, sl] = ((yn + bonus) * gate_s[rows, sl]).astype(y_ref.dtype)
        return carry

    lax.fori_loop(0, tt // c, scan, 0)
    _keep_halo(pbuf, tt)


def _rwkv(x1, gmix, win, mu, w0, a0, k_k, k_a, w2a2, g2, r_k, gn_w, gn_b):
    b, t, _ = x1.shape
    tt = RWKV_TILE
    nc = tt // RWKV_CHUNK
    npair = RWKV_HEADS // 2
    slab = 2 * RWKV_N
    consts = (gmix, win, mu, w0, a0, k_k, k_a, w2a2, g2, r_k, gn_w, gn_b)
    wide = pltpu.VMEM((tt, RWKV_W), F32)
    return pl.pallas_call(
        _rwkv_kernel,
        out_shape=jax.ShapeDtypeStruct((b, t, RWKV_W), BF16),
        grid=(b, t // tt),
        in_specs=[pl.BlockSpec((None, tt, D_MODEL), lambda i, j: (i, j, 0))] + [_const_spec(c.shape) for c in consts],
        out_specs=pl.BlockSpec((None, tt, RWKV_W), lambda i, j: (i, j, 0)),
        scratch_shapes=[pltpu.VMEM((tt + HALO, RW_COLS), F32), wide, wide, wide, wide, wide, wide, wide,
                        pltpu.VMEM((npair, slab, slab), F32),
                        pltpu.VMEM((nc, npair, 2 * RWKV_CHUNK, slab), BF16),
                        pltpu.VMEM((nc, npair, RWKV_CHUNK, slab), F32),
                        pltpu.VMEM((nc, npair, RWKV_CHUNK, slab), F32),
                        pltpu.VMEM((nc, npair, RWKV_CHUNK, slab), BF16),
                        pltpu.VMEM((nc, npair, slab, 2 * RWKV_CHUNK), BF16),
                        pltpu.VMEM((nc, npair, RWKV_CHUNK, slab), BF16),
                        pltpu.VMEM((nc, npair, slab, slab), F32)],
        compiler_params=pltpu.CompilerParams(dimension_semantics=("arbitrary", "arbitrary"),
                                             vmem_limit_bytes=VMEM_LIMIT),
        name="rwkv",
    )(x1, *consts)


def _tail_kernel(x_ref, yg_ref, yr_ref, p_ref, wout_ref, gmp_ref, gpre_ref, wg_ref, wu_ref, wd_ref, gpost_ref,
                 gple_ref, wpg_ref, wpp_ref, gplep_ref, o_ref, a_scr):
    m = (jnp.dot(yg_ref[...], wout_ref[0:GDN_W, :], preferred_element_type=F32)
         + jnp.dot(yr_ref[...], wout_ref[GDN_W:GDN_W + RWKV_W, :], preferred_element_type=F32))
    x = x_ref[...] + _rms(m, gmp_ref[...])
    x = _swiglu_ffn(x, gpre_ref[...], wg_ref, wu_ref, wd_ref, gpost_ref[...], a_scr)
    gate = _sigmoid(jnp.dot(_rms(x, gple_ref[...]).astype(BF16), wpg_ref[...], preferred_element_type=F32))
    emb = jnp.dot(p_ref[...].astype(BF16), wpp_ref[...], preferred_element_type=F32)
    o_ref[...] = x + _rms(gate * emb, gplep_ref[...])


def _tail(x1, yg, yr, p, wout, gmp, gpre, wg, wu, wd, gpost, gple, wpg, wpp, gplep, tm=512):
    n = x1.shape[0]
    row = pl.BlockSpec((tm, D_MODEL), lambda i: (i, 0))
    half = pl.BlockSpec((tm, GDN_W), lambda i: (i, 0))
    ple = pl.BlockSpec((tm, D_PLE), lambda i: (i, 0))
    consts = (wout, gmp, gpre, wg, wu, wd, gpost, gple, wpg, wpp, gplep)
    return pl.pallas_call(
        _tail_kernel,
        out_shape=jax.ShapeDtypeStruct(x1.shape, F32),
        grid=(n // tm,),
        in_specs=[row, half, half, ple] + [_const_spec(c.shape) for c in consts],
        out_specs=row,
        scratch_shapes=[pltpu.VMEM((tm, D_FF), BF16)],
        compiler_params=pltpu.CompilerParams(dimension_semantics=("arbitrary",), vmem_limit_bytes=VMEM_LIMIT),
        name="tail",
    )(x1, yg, yr, p, *consts)


def _row(v):
    return v.reshape(1, -1).astype(F32)


def _layer(x, p, ffn1_norm_pre, ffn1_w_gate, ffn1_w_up, ffn1_w_down, ffn1_norm_post, mix_norm_pre, w_in,
           gdn_conv_w, gdn_a_log, gdn_dt_bias, gdn_norm_w, rwkv_mu, rwkv_w0, rwkv_w2, rwkv_a0, rwkv_a2, rwkv_g2,
           rwkv_k_k, rwkv_k_a, rwkv_r_k, rwkv_gn_w, rwkv_gn_b, w_out, mix_norm_post, ffn2_norm_pre, ffn2_w_gate,
           ffn2_w_up, ffn2_w_down, ffn2_norm_post, ple_norm_pre, ple_w_gate, ple_w_proj, ple_norm_post):
    b, t, d = x.shape
    n = b * t
    x1 = _ffn(x.reshape(n, d), _row(ffn1_norm_pre), ffn1_w_gate.astype(BF16), ffn1_w_up.astype(BF16),
              ffn1_w_down.astype(BF16), _row(ffn1_norm_post)).reshape(b, t, d)

    n_main = 4 * GDN_W
    n_bg = 2 * GDN_HEADS
    win_gdn = jnp.concatenate([w_in[:, :n_main + n_bg], jnp.zeros((d, V7X_LANES - n_bg), w_in.dtype)],
                              axis=1).astype(BF16)
    win_rw = w_in[:, n_main + n_bg:].astype(BF16)
    pad_bg = lambda vec: jnp.concatenate([jnp.zeros((GDN_HEADS,), F32), vec.astype(F32),
                                          jnp.zeros((V7X_LANES - n_bg,), F32)]).reshape(1, V7X_LANES)
    zero_l = jnp.zeros((RWKV_N, RWKV_W), F32)
    w2a2 = jnp.concatenate([jnp.concatenate([rwkv_w2.astype(F32), zero_l], axis=1),
                            jnp.concatenate([zero_l, rwkv_a2.astype(F32)], axis=1)], axis=0).astype(BF16)
    gmix = _row(mix_norm_pre)
    y_gdn = _gdn(x1, gmix, win_gdn, gdn_conv_w.astype(F32), pad_bg(gdn_a_log), pad_bg(gdn_dt_bias), _row(gdn_norm_w))
    y_rwkv = _rwkv(x1, gmix, win_rw, _row(rwkv_mu), _row(rwkv_w0), _row(rwkv_a0), _row(rwkv_k_k), _row(rwkv_k_a),
                   w2a2, rwkv_g2.astype(BF16), _row(rwkv_r_k), _row(rwkv_gn_w), _row(rwkv_gn_b))
    out = _tail(x1.reshape(n, d), y_gdn.reshape(n, GDN_W), y_rwkv.reshape(n, RWKV_W), p.reshape(n, D_PLE),
                w_out.astype(BF16), _row(mix_norm_post), _row(ffn2_norm_pre), ffn2_w_gate.astype(BF16),
                ffn2_w_up.astype(BF16), ffn2_w_down.astype(BF16), _row(ffn2_norm_post), _row(ple_norm_pre),
                ple_w_gate.astype(BF16), ple_w_proj.astype(BF16), _row(ple_norm_post))
    return out.reshape(b, t, d)


def kernel(x, p, ffn1_norm_pre, ffn1_w_gate, ffn1_w_up, ffn1_w_down, ffn1_norm_post, mix_norm_pre, w_in, gdn_conv_w, gdn_a_log, gdn_dt_bias, gdn_norm_w, rwkv_mu, rwkv_w0, rwkv_w2, rwkv_a0, rwkv_a2, rwkv_g2, rwkv_k_k, rwkv_k_a, rwkv_r_k, rwkv_gn_w, rwkv_gn_b, w_out, mix_norm_post, ffn2_norm_pre, ffn2_w_gate, ffn2_w_up, ffn2_w_down, ffn2_norm_post, ple_norm_pre, ple_w_gate, ple_w_proj, ple_norm_post):
    return _layer(x, p[0], ffn1_norm_pre[0], ffn1_w_gate[0], ffn1_w_up[0], ffn1_w_down[0], ffn1_norm_post[0],
                  mix_norm_pre[0], w_in[0], gdn_conv_w[0], gdn_a_log[0], gdn_dt_bias[0], gdn_norm_w[0], rwkv_mu[0],
                  rwkv_w0[0], rwkv_w2[0], rwkv_a0[0], rwkv_a2[0], rwkv_g2[0], rwkv_k_k[0], rwkv_k_a[0], rwkv_r_k[0],
                  rwkv_gn_w[0], rwkv_gn_b[0], w_out[0], mix_norm_post[0], ffn2_norm_pre[0], ffn2_w_gate[0],
                  ffn2_w_up[0], ffn2_w_down[0], ffn2_norm_post[0], ple_norm_pre[0], ple_w_gate[0], ple_w_proj[0],
                  ple_norm_post[0])
```

```python
import functools

import jax
import jax.numpy as jnp
from jax import lax
from jax.experimental import pallas as pl
from jax.experimental.pallas import tpu as pltpu

F32 = jnp.float32
BF16 = jnp.bfloat16

D_MODEL = 1024
D_FF = 2816
D_PLE = 256
GDN_HEADS = 4
GDN_D = 128
GDN_CONV = 4
RWKV_HEADS = 8
RWKV_N = 64
RWKV_W = RWKV_HEADS * RWKV_N
GDN_W = GDN_HEADS * GDN_D
LORA_WA = 128
LORA_G = 128
NORM_EPS = 1e-6
GN_EPS = 64e-5
L2_EPS = 1e-6

V7X_LANES = 128

GCOL_Z = 3 * GDN_W
GCOL_BG = 4 * GDN_W
GDN_COLS = GCOL_BG + V7X_LANES
RW_COLS = 3 * RWKV_W + LORA_WA + LORA_G

GDN_CHUNK = 128
RWKV_CHUNK = 64
HALO = 8
GDN_TILE = 512
RWKV_TILE = 256

FF_CHUNKS = ((0, 768), (768, 768), (1536, 768), (2304, 512))
VMEM_LIMIT = 56 * 1024 * 1024


def _dot(a, b):
    return jnp.dot(a.astype(BF16), b.astype(BF16), preferred_element_type=F32)


def _dot_nt(a, b):
    return lax.dot_general(a.astype(BF16), b.astype(BF16), (((1,), (1,)), ((), ())),
                           preferred_element_type=F32)


def _rms(x, g):
    return x * lax.rsqrt(jnp.mean(x * x, axis=-1, keepdims=True) + NORM_EPS) * g


def _sigmoid(x):
    return 1.0 / (1.0 + jnp.exp(-x))


def _silu(x):
    h = 0.5 * x
    return h + h * jnp.tanh(h)


def _softplus(x):
    return jnp.maximum(x, 0.0) + jnp.log(1.0 + jnp.exp(-jnp.abs(x)))


def _split3(x):
    hi = x.astype(BF16)
    r1 = x - hi.astype(F32)
    mid = r1.astype(BF16)
    lo = (r1 - mid.astype(F32)).astype(BF16)
    return hi, mid, lo


def _cumsum_rows(tri, x):
    hi, mid, lo = _split3(x)
    return (jnp.dot(tri, hi, preferred_element_type=F32)
            + jnp.dot(tri, mid, preferred_element_type=F32)
            + jnp.dot(tri, lo, preferred_element_type=F32))


def _odd_blocks(x, b):
    return jnp.concatenate([x[s:s + b] for s in range(b, x.shape[0], 2 * b)], axis=0)


def _with_odd_blocks(base, odd, b):
    parts = []
    for j, s in enumerate(range(0, base.shape[0], 2 * b)):
        parts.append(base[s:s + b])
        parts.append(odd[j * b:(j + 1) * b])
    return jnp.concatenate(parts, axis=0)


def _unit_lower_inverses(lmats, top):
    n = lmats[0].shape[0]
    row = lax.broadcasted_iota(jnp.int32, (n, n), 0)
    col = lax.broadcasted_iota(jnp.int32, (n, n), 1)
    lower = row > col
    eye = jnp.where(row == col, 1.0, 0.0)
    xs = [eye - jnp.where(lower & ((row // 2) == (col // 2)), lm, 0.0) for lm in lmats]
    b = 2
    while b < top:
        sub = lower & ((row // (2 * b)) == (col // (2 * b))) & ((row // b) != (col // b))
        cs = [jnp.where(sub, lm, 0.0) for lm in lmats]
        if b % 8:
            ys = [_dot(c, x) for c, x in zip(cs, xs)]
            yield
            xs = [x - _dot(x, y) for x, y in zip(xs, ys)]
            yield
        else:
            zero = jnp.zeros((n, n), F32)
            ys = [_with_odd_blocks(zero, _dot(_odd_blocks(c, b), x), b) for c, x in zip(cs, xs)]
            yield
            xs = [_with_odd_blocks(x, _odd_blocks(x, b) - _dot(_odd_blocks(x, b), y), b) for x, y in zip(xs, ys)]
            yield
        b *= 2
    return xs


def _interleave(main, side):
    side_live = True
    while True:
        if side_live:
            try:
                next(side)
            except StopIteration:
                side_live = False
        try:
            next(main)
        except StopIteration:
            break
    if side_live:
        for _ in side:
            pass


def _swiglu_ffn(x, gpre, wg_ref, wu_ref, wd_ref, gpost, a_scr):
    h = _rms(x, gpre).astype(BF16)
    for s, w in FF_CHUNKS:
        g = jnp.dot(h, wg_ref[:, s:s + w], preferred_element_type=F32)
        u = jnp.dot(h, wu_ref[:, s:s + w], preferred_element_type=F32)
        a_scr[:, s:s + w] = (_silu(g) * u).astype(BF16)
    f = jnp.dot(a_scr[...], wd_ref[...], preferred_element_type=F32)
    return x + 0.5 * _rms(f, gpost)


def _const_spec(shape):
    return pl.BlockSpec(shape, lambda *_: (0,) * len(shape), pipeline_mode=pl.Buffered(1))


def _project(x_ref, gmix_ref, win_ref, pbuf):
    tile = x_ref.shape[0]
    h = _rms(x_ref[...], gmix_ref[...]).astype(BF16)
    pbuf[HALO:HALO + tile, :] = jnp.dot(h, win_ref[...], preferred_element_type=F32)


def _tile_clock(n_t):
    s = pl.program_id(0)
    cur = lax.rem(s, jnp.int32(2))
    t_cur = lax.rem(s, jnp.int32(n_t))
    t_prev = lax.rem(s + jnp.int32(n_t - 1), jnp.int32(n_t))
    return s == 0, t_cur == 0, t_prev == 0, cur, 1 - cur


def _keep_halo(pbuf, tile):
    pbuf[0:HALO, :] = pbuf[tile:tile + HALO, :]


def _ffn_kernel(x_ref, gpre_ref, wg_ref, wu_ref, wd_ref, gpost_ref, o_ref, a_scr):
    o_ref[...] = _swiglu_ffn(x_ref[...], gpre_ref[...], wg_ref, wu_ref, wd_ref, gpost_ref[...], a_scr)


def _ffn(x2d, gpre, wg, wu, wd, gpost, tm=512):
    n = x2d.shape[0]
    row = pl.BlockSpec((tm, D_MODEL), lambda i: (i, 0))
    return pl.pallas_call(
        _ffn_kernel,
        out_shape=jax.ShapeDtypeStruct(x2d.shape, F32),
        grid=(n // tm,),
        in_specs=[row, _const_spec((1, D_MODEL)), _const_spec((D_MODEL, D_FF)), _const_spec((D_MODEL, D_FF)),
                  _const_spec((D_FF, D_MODEL)), _const_spec((1, D_MODEL))],
        out_specs=row,
        scratch_shapes=[pltpu.VMEM((tm, D_FF), BF16)],
        compiler_params=pltpu.CompilerParams(dimension_semantics=("arbitrary",), vmem_limit_bytes=VMEM_LIMIT),
        name="ffn",
    )(x2d, gpre, wg, wu, wd, gpost)


def _gdn_kernel(x_ref, gmix_ref, win_ref, convw_ref, alog_ref, dtb_ref, nw_ref, y_ref,
                pbuf, q_s, k_s, v_s, bg_s, s_ref, z_scr, zl_scr, u_scr, aqk_scr, kdt_scr, egl_scr, *, n_t):
    tt = x_ref.shape[0]
    c = GDN_CHUNK
    nh = GDN_HEADS
    nc = tt // c
    first_step, seq_start, prev_seq_start, cur, prev = _tile_clock(n_t)

    @pl.when(first_step)
    def _():
        for ref in (z_scr, zl_scr, u_scr, aqk_scr, kdt_scr, egl_scr):
            ref[1] = jnp.zeros(ref.shape[1:], ref.dtype)

    @pl.when(first_step | prev_seq_start)
    def _():
        s_ref[...] = jnp.zeros(s_ref.shape, F32)

    @pl.when(seq_start)
    def _():
        pbuf[0:HALO, :] = jnp.zeros((HALO, pbuf.shape[1]), F32)

    row = lax.broadcasted_iota(jnp.int32, (c, c), 0)
    col = lax.broadcasted_iota(jnp.int32, (c, c), 1)
    incl = row >= col
    strict = row > col
    tri = jnp.where(incl, 1.0, 0.0).astype(BF16)

    def prepare():
        _project(x_ref, gmix_ref, win_ref, pbuf)
        yield
        for grp, dst in ((0, q_s), (1, k_s), (2, v_s)):
            for hd in range(nh):
                c0 = grp * GDN_W + hd * GDN_D
                xe = pbuf[0:HALO + tt, c0:c0 + GDN_D]
                acc = xe * convw_ref[0:1, c0:c0 + GDN_D]
                for j in range(1, GDN_CONV):
                    acc = pltpu.roll(acc, 1, axis=0) + xe * convw_ref[j:j + 1, c0:c0 + GDN_D]
                y = _silu(acc[HALO:])
                if grp < 2:
                    y = y * lax.rsqrt(jnp.sum(y * y, axis=-1, keepdims=True) + L2_EPS)
                if grp == 0:
                    y = y * (GDN_D ** -0.5)
                dst[:, hd * GDN_D:(hd + 1) * GDN_D] = y
        lane = lax.broadcasted_iota(jnp.int32, (tt, V7X_LANES), 1)
        pb = pbuf[HALO:HALO + tt, GCOL_BG:GCOL_BG + V7X_LANES]
        gdec = -jnp.exp(alog_ref[...]) * _softplus(pb + dtb_ref[...])
        bg_s[...] = jnp.where(lane < nh, _sigmoid(pb), jnp.where(lane < 2 * nh, gdec, 0.0))
        z_scr[cur] = pbuf[HALO:HALO + tt, GCOL_Z:GCOL_Z + GDN_W]
        pbuf[0:HALO, :] = pbuf[tt:tt + HALO, :]

        units = []
        for ci in range(nc):
            rows = slice(ci * c, (ci + 1) * c)
            bg = bg_s[rows, :]
            gc = _cumsum_rows(tri, bg)
            gct = gc.T
            for hd in range(nh):
                sl = slice(hd * GDN_D, (hd + 1) * GDN_D)
                gcol = gc[:, nh + hd:nh + hd + 1]
                grow = gct[nh + hd:nh + hd + 1, :]
                glast = gc[c - 1:c, nh + hd:nh + hd + 1]
                beta = bg[:, hd:hd + 1]
                k = k_s[rows, sl]
                units.append(dict(
                    ci=ci, hd=hd, gcol=gcol, glast=glast, k=k, q=q_s[rows, sl], kb=k * beta,
                    vb=v_s[rows, sl] * beta, eg=jnp.exp(gcol),
                    dec=jnp.where(incl, jnp.exp(jnp.where(incl, gcol - grow, 0.0)), 0.0)))
        yield
        kqs = [_dot_nt(jnp.concatenate([un["kb"], un["q"]], axis=0), un["k"]) for un in units]
        yield
        tinvs = yield from _unit_lower_inverses(
            [jnp.where(strict, kq[:c] * un["dec"], 0.0) for kq, un in zip(kqs, units)], c)
        uws = [_dot(tinv, jnp.concatenate([un["vb"], un["kb"] * un["eg"]], axis=1))
               for tinv, un in zip(tinvs, units)]
        yield
        for un, kq, uw in zip(units, kqs, uws):
            ci, hd = un["ci"], un["hd"]
            zl_scr[cur, ci, hd] = jnp.concatenate([uw[:, GDN_D:], un["q"] * un["eg"]], axis=0).astype(BF16)
            u_scr[cur, ci, hd] = uw[:, :GDN_D]
            aqk_scr[cur, ci, hd] = (kq[c:] * un["dec"]).astype(BF16)
            kdt_scr[cur, ci, hd] = (un["k"] * jnp.exp(un["glast"] - un["gcol"])).T.astype(BF16)
            egl_scr[cur, ci, hd] = jnp.broadcast_to(jnp.exp(un["glast"]), (8, GDN_D))

    def finish():
        for ci in range(nc):
            rows = slice(ci * c, (ci + 1) * c)
            zss = [_dot(zl_scr[prev, ci, hd], s_ref[hd]) for hd in range(nh)]
            yield
            v_news = [u_scr[prev, ci, hd] - zss[hd][:c] for hd in range(nh)]
            outs = [zss[hd][c:] + _dot(aqk_scr[prev, ci, hd], v_news[hd]) for hd in range(nh)]
            upds = [_dot(kdt_scr[prev, ci, hd], v_news[hd]) for hd in range(nh)]
            for hd in range(nh):
                s_ref[hd] = s_ref[hd] * egl_scr[prev, ci, hd][0:1, :] + upds[hd]
            for hd in range(nh):
                sl = slice(hd * GDN_D, (hd + 1) * GDN_D)
                o = outs[hd]
                o = o * lax.rsqrt(jnp.mean(o * o, axis=-1, keepdims=True) + NORM_EPS) * nw_ref[...]
                y_ref[rows, sl] = (o * _silu(z_scr[prev, rows, sl])).astype(y_ref.dtype)
            yield

    _interleave(prepare(), finish())


def _flat_tile_specs(n_t, n_tiles, tile, width_in, width_out):
    def in_map(s):
        i = jnp.minimum(s, n_tiles - 1)
        return (i // n_t, i % n_t, 0)

    def out_map(s):
        i = jnp.maximum(s - 1, 0)
        return (i // n_t, i % n_t, 0)

    return pl.BlockSpec((None, tile, width_in), in_map), pl.BlockSpec((None, tile, width_out), out_map)


def _gdn(x1, gmix, win, convw, alog, dtb, nw):
    b, t, _ = x1.shape
    tt = GDN_TILE
    nc = tt // GDN_CHUNK
    n_t = t // tt
    consts = (gmix, win, convw, alog, dtb, nw)
    in_spec, out_spec = _flat_tile_specs(n_t, b * n_t, tt, D_MODEL, GDN_W)
    return pl.pallas_call(
        functools.partial(_gdn_kernel, n_t=n_t),
        out_shape=jax.ShapeDtypeStruct((b, t, GDN_W), BF16),
        grid=(b * n_t + 1,),
        in_specs=[in_spec] + [_const_spec(c.shape) for c in consts],
        out_specs=out_spec,
        scratch_shapes=[pltpu.VMEM((tt + HALO, GDN_COLS), F32),
                        pltpu.VMEM((tt, GDN_W), F32), pltpu.VMEM((tt, GDN_W), F32), pltpu.VMEM((tt, GDN_W), F32),
                        pltpu.VMEM((tt, V7X_LANES), F32),
                        pltpu.VMEM((GDN_HEADS, GDN_D, GDN_D), F32),
                        pltpu.VMEM((2, tt, GDN_W), F32),
                        pltpu.VMEM((2, nc, GDN_HEADS, 2 * GDN_CHUNK, GDN_D), BF16),
                        pltpu.VMEM((2, nc, GDN_HEADS, GDN_CHUNK, GDN_D), F32),
                        pltpu.VMEM((2, nc, GDN_HEADS, GDN_CHUNK, GDN_CHUNK), BF16),
                        pltpu.VMEM((2, nc, GDN_HEADS, GDN_D, GDN_CHUNK), BF16),
                        pltpu.VMEM((2, nc, GDN_HEADS, 8, GDN_D), F32)],
        compiler_params=pltpu.CompilerParams(dimension_semantics=("arbitrary",), vmem_limit_bytes=VMEM_LIMIT),
        name="gdn",
    )(x1, *consts)


def _rwkv_kernel(x_ref, gmix_ref, win_ref, mu_ref, w0_ref, a0_ref, kkw_ref, kaw_ref, w2a2_ref, g2_ref,
                 rk_ref, gnw_ref, gnb_ref, y_ref,
                 pbuf, lw_s, kk_s, a_s, r_s, k_s, v_s, gate_s,
                 s_ref, zl_scr, ut_scr, yv_scr, arb_scr, kbt_scr, vb_scr, dm_scr, *, n_t):
    tt = x_ref.shape[0]
    c = RWKV_CHUNK
    n = RWKV_N
    npair = RWKV_HEADS // 2
    nc = tt // c
    first_step, seq_start, prev_seq_start, cur, prev = _tile_clock(n_t)

    @pl.when(first_step)
    def _():
        for ref in (r_s, k_s, v_s, gate_s, zl_scr, ut_scr, yv_scr, arb_scr, kbt_scr, vb_scr, dm_scr):
            ref[1] = jnp.zeros(ref.shape[1:], ref.dtype)

    @pl.when(first_step | prev_seq_start)
    def _():
        s_ref[...] = jnp.zeros(s_ref.shape, F32)

    @pl.when(seq_start)
    def _():
        pbuf[0:HALO, :] = jnp.zeros((HALO, pbuf.shape[1]), F32)

    lane_t = lax.broadcasted_iota(jnp.int32, (tt, V7X_LANES), 1)
    lo_half = lane_t < n

    def shifted(c0, width):
        now = pbuf[HALO:HALO + tt, c0:c0 + width]
        before = pbuf[HALO - 1:HALO - 1 + tt, c0:c0 + width]
        return now + (before - now) * mu_ref[:, c0:c0 + width]

    def pointwise():
        lwa = shifted(3 * RWKV_W, LORA_WA)
        lwa = jnp.where(lo_half, jnp.tanh(lwa), lwa)
        wa = jnp.dot(lwa.astype(BF16), w2a2_ref[...], preferred_element_type=F32)
        lg = shifted(3 * RWKV_W + LORA_WA, LORA_G)
        gate_s[cur] = jnp.dot(_sigmoid(lg).astype(BF16), g2_ref[...], preferred_element_type=F32)
        for p in range(npair):
            sl = slice(p * V7X_LANES, (p + 1) * V7X_LANES)
            w_log = -_softplus(-(w0_ref[:, sl] + wa[:, sl])) - 0.5
            lw_s[:, sl] = -jnp.exp(w_log)
            a = _sigmoid(a0_ref[:, sl] + wa[:, RWKV_W + p * V7X_LANES:RWKV_W + (p + 1) * V7X_LANES])
            a_s[:, sl] = a
            r_s[cur, :, sl] = shifted(p * V7X_LANES, V7X_LANES)
            kr = shifted(RWKV_W + p * V7X_LANES, V7X_LANES)
            v_s[cur, :, sl] = shifted(2 * RWKV_W + p * V7X_LANES, V7X_LANES)
            kx = kr * kkw_ref[:, sl]
            sq = kx * kx
            s_lo = jnp.sum(jnp.where(lo_half, sq, 0.0), axis=-1, keepdims=True)
            s_hi = jnp.sum(jnp.where(lo_half, 0.0, sq), axis=-1, keepdims=True)
            kk_s[:, sl] = kx * lax.rsqrt(jnp.where(lo_half, s_lo, s_hi) + L2_EPS)
            k_s[cur, :, sl] = kr * (1.0 + (a - 1.0) * kaw_ref[:, sl])
        pbuf[0:HALO, :] = pbuf[tt:tt + HALO, :]

    row_c = lax.broadcasted_iota(jnp.int32, (c, c), 0)
    col_c = lax.broadcasted_iota(jnp.int32, (c, c), 1)
    tri = jnp.where(row_c >= col_c, 1.0, 0.0).astype(BF16)
    row = lax.broadcasted_iota(jnp.int32, (c, 2 * n), 0)
    lane = lax.broadcasted_iota(jnp.int32, (c, 2 * n), 1)
    m0 = lane < n
    jj = jnp.where(m0, lane, lane - n)
    strict = row > jj
    incl = row >= jj
    row2 = lax.broadcasted_iota(jnp.int32, (2 * n, 2 * n), 0)
    lane2 = lax.broadcasted_iota(jnp.int32, (2 * n, 2 * n), 1)
    same_head = (row2 < n) == (lane2 < n)

    def sel0(x):
        return jnp.where(m0, x, 0.0)

    def sel1(x):
        return jnp.where(m0, 0.0, x)

    def head_sum(t):
        s_lo = jnp.sum(sel0(t), axis=-1, keepdims=True)
        s_hi = jnp.sum(sel1(t), axis=-1, keepdims=True)
        return jnp.where(m0, s_lo, s_hi)

    def prepare():
        _project(x_ref, gmix_ref, win_ref, pbuf)
        yield
        pointwise()
        units = []
        for ci in range(nc):
            rows = slice(ci * c, (ci + 1) * c)
            g_all = _cumsum_rows(tri, lw_s[rows, :])
            for p in range(npair):
                sl = slice(p * 2 * n, (p + 1) * 2 * n)
                g = g_all[:, sl]
                gmid = g[c // 2 - 1:c // 2, :]
                gend = g[c - 1:c, :]
                e_inv = jnp.exp(gmid - g)
                e_end = jnp.exp(gend - g)
                egm = jnp.exp(gmid)
                k = k_s[cur, rows, sl]
                kk = kk_s[rows, sl]
                b = kk * a_s[rows, sl]
                kap_t = kk * jnp.exp(g - lw_s[rows, sl] - gmid)
                r_t = r_s[cur, rows, sl] * jnp.exp(g - gmid)
                units.append(dict(ci=ci, p=p, v=v_s[cur, rows, sl], kap_t=kap_t, r_t=r_t, b_t=b * e_inv,
                                  k_t=k * e_inv, kb=jnp.concatenate([k * e_end, -(b * e_end)], axis=0), egm=egm,
                                  egend=jnp.exp(gend)))
        yield
        a0s = [_dot_nt(jnp.concatenate([sel0(un["kap_t"]), sel0(un["r_t"])], axis=0),
                       jnp.concatenate([un["b_t"], un["k_t"]], axis=0)) for un in units]
        yield
        a1s = [_dot_nt(jnp.concatenate([sel1(un["kap_t"]), sel1(un["r_t"])], axis=0),
                       jnp.concatenate([un["k_t"], un["b_t"]], axis=0)) for un in units]
        yield
        lmats = [jnp.concatenate([jnp.where(strict & m0, a0[:c], 0.0), jnp.where(strict & (~m0), a1[:c], 0.0)],
                                 axis=0) for a0, a1 in zip(a0s, a1s)]
        avys = []
        for un, a0, a1 in zip(units, a0s, a1s):
            a_uk = jnp.where(strict, jnp.where(m0, a1[:c], a0[:c]), 0.0)
            a_rk = jnp.where(incl, jnp.where(m0, a1[c:], a0[c:]), 0.0)
            v_sw = jnp.concatenate([sel1(un["v"]), sel0(un["v"])], axis=0)
            avys.append(_dot(jnp.concatenate([a_uk, a_rk], axis=0), v_sw))
        yield
        tbds = yield from _unit_lower_inverses(lmats, c)
        wus = []
        for un, tbd, avy in zip(units, tbds, avys):
            tcat = tbd[:c] + tbd[c:]
            kap_b = un["kap_t"] * un["egm"]
            av = avy[:c]
            x = jnp.concatenate([jnp.concatenate([sel0(kap_b), sel0(av)], axis=1),
                                 jnp.concatenate([sel1(kap_b), sel1(av)], axis=1)], axis=0)
            wus.append(_dot(tcat, x))
        yield
        for un, a0, a1, avy, wu in zip(units, a0s, a1s, avys, wus):
            ci, p = un["ci"], un["p"]
            zl_scr[cur, ci, p] = jnp.concatenate([wu[:, :2 * n], un["r_t"] * un["egm"]], axis=0).astype(BF16)
            ut_scr[cur, ci, p] = wu[:, 2 * n:]
            yv_scr[cur, ci, p] = avy[c:]
            arb_scr[cur, ci, p] = jnp.where(incl, jnp.where(m0, a0[c:], a1[c:]), 0.0).astype(BF16)
            kbt_scr[cur, ci, p] = un["kb"].T.astype(BF16)
            vb_scr[cur, ci, p] = un["v"].astype(BF16)
            dm_scr[cur, ci, p] = jnp.broadcast_to(un["egend"], (2 * n, 2 * n)).T

    def finish():
        for ci in range(nc):
            rows = slice(ci * c, (ci + 1) * c)
            zss = [_dot(zl_scr[prev, ci, p], s_ref[p]) for p in range(npair)]
            yield
            us = [zss[p][:c] + ut_scr[prev, ci, p] for p in range(npair)]
            ys = [zss[p][c:] - _dot(arb_scr[prev, ci, p], jnp.concatenate([sel0(us[p]), sel1(us[p])], axis=0))
                  + yv_scr[prev, ci, p] for p in range(npair)]
            upds = [jnp.dot(kbt_scr[prev, ci, p], jnp.concatenate([vb_scr[prev, ci, p], us[p].astype(BF16)], axis=0),
                            preferred_element_type=F32) for p in range(npair)]
            for p in range(npair):
                s_ref[p] = s_ref[p] * dm_scr[prev, ci, p] + jnp.where(same_head, upds[p], 0.0)
            for p in range(npair):
                sl = slice(p * 2 * n, (p + 1) * 2 * n)
                y = ys[p]
                mean = head_sum(y) * (1.0 / n)
                dlt = y - mean
                var = head_sum(dlt * dlt) * (1.0 / n)
                yn = dlt * lax.rsqrt(var + GN_EPS) * gnw_ref[:, sl] + gnb_ref[:, sl]
                bonus = head_sum(r_s[prev, rows, sl] * k_s[prev, rows, sl] * rk_ref[:, sl]) * v_s[prev, rows, sl]
                y_ref[rows, sl] = ((yn + bonus) * gate_s[prev, rows, sl]).astype(y_ref.dtype)
            yield

    _interleave(prepare(), finish())


def _rwkv(x1, gmix, win, mu, w0, a0, k_k, k_a, w2a2, g2, r_k, gn_w, gn_b):
    b, t, _ = x1.shape
    tt = RWKV_TILE
    nc = tt // RWKV_CHUNK
    n_t = t // tt
    npair = RWKV_HEADS // 2
    slab = 2 * RWKV_N
    consts = (gmix, win, mu, w0, a0, k_k, k_a, w2a2, g2, r_k, gn_w, gn_b)
    wide = pltpu.VMEM((tt, RWKV_W), F32)
    wide2 = pltpu.VMEM((2, tt, RWKV_W), F32)
    in_spec, out_spec = _flat_tile_specs(n_t, b * n_t, tt, D_MODEL, RWKV_W)
    return pl.pallas_call(
        functools.partial(_rwkv_kernel, n_t=n_t),
        out_shape=jax.ShapeDtypeStruct((b, t, RWKV_W), BF16),
        grid=(b * n_t + 1,),
        in_specs=[in_spec] + [_const_spec(c.shape) for c in consts],
        out_specs=out_spec,
        scratch_shapes=[pltpu.VMEM((tt + HALO, RW_COLS), F32), wide, wide, wide, wide2, wide2, wide2, wide2,
                        pltpu.VMEM((npair, slab, slab), F32),
                        pltpu.VMEM((2, nc, npair, 2 * RWKV_CHUNK, slab), BF16),
                        pltpu.VMEM((2, nc, npair, RWKV_CHUNK, slab), F32),
                        pltpu.VMEM((2, nc, npair, RWKV_CHUNK, slab), F32),
                        pltpu.VMEM((2, nc, npair, RWKV_CHUNK, slab), BF16),
                        pltpu.VMEM((2, nc, npair, slab, 2 * RWKV_CHUNK), BF16),
                        pltpu.VMEM((2, nc, npair, RWKV_CHUNK, slab), BF16),
                        pltpu.VMEM((2, nc, npair, slab, slab), F32)],
        compiler_params=pltpu.CompilerParams(dimension_semantics=("arbitrary",), vmem_limit_bytes=VMEM_LIMIT),
        name="rwkv",
    )(x1, *consts)


def _tail_kernel(x_ref, yg_ref, yr_ref, p_ref, wout_ref, gmp_ref, gpre_ref, wg_ref, wu_ref, wd_ref, gpost_ref,
                 gple_ref, wpg_ref, wpp_ref, gplep_ref, o_ref, a_scr):
    m = (jnp.dot(yg_ref[...], wout_ref[0:GDN_W, :], preferred_element_type=F32)
         + jnp.dot(yr_ref[...], wout_ref[GDN_W:GDN_W + RWKV_W, :], preferred_element_type=F32))
    x = x_ref[...] + _rms(m, gmp_ref[...])
    x = _swiglu_ffn(x, gpre_ref[...], wg_ref, wu_ref, wd_ref, gpost_ref[...], a_scr)
    gate = _sigmoid(jnp.dot(_rms(x, gple_ref[...]).astype(BF16), wpg_ref[...], preferred_element_type=F32))
    emb = jnp.dot(p_ref[...].astype(BF16), wpp_ref[...], preferred_element_type=F32)
    o_ref[...] = x + _rms(gate * emb, gplep_ref[...])


def _tail(x1, yg, yr, p, wout, gmp, gpre, wg, wu, wd, gpost, gple, wpg, wpp, gplep, tm=512):
    n = x1.shape[0]
    row = pl.BlockSpec((tm, D_MODEL), lambda i: (i, 0))
    half = pl.BlockSpec((tm, GDN_W), lambda i: (i, 0))
    ple = pl.BlockSpec((tm, D_PLE), lambda i: (i, 0))
    consts = (wout, gmp, gpre, wg, wu, wd, gpost, gple, wpg, wpp, gplep)
    return pl.pallas_call(
        _tail_kernel,
        out_shape=jax.ShapeDtypeStruct(x1.shape, F32),
        grid=(n // tm,),
        in_specs=[row, half, half, ple] + [_const_spec(c.shape) for c in consts],
        out_specs=row,
        scratch_shapes=[pltpu.VMEM((tm, D_FF), BF16)],
        compiler_params=pltpu.CompilerParams(dimension_semantics=("arbitrary",), vmem_limit_bytes=VMEM_LIMIT),
        name="tail",
    )(x1, yg, yr, p, *consts)


def _row(v):
    return v.reshape(1, -1).astype(F32)


def _layer(x, p, ffn1_norm_pre, ffn1_w_gate, ffn1_w_up, ffn1_w_down, ffn1_norm_post, mix_norm_pre, w_in,
           gdn_conv_w, gdn_a_log, gdn_dt_bias, gdn_norm_w, rwkv_mu, rwkv_w0, rwkv_w2, rwkv_a0, rwkv_a2, rwkv_g2,
           rwkv_k_k, rwkv_k_a, rwkv_r_k, rwkv_gn_w, rwkv_gn_b, w_out, mix_norm_post, ffn2_norm_pre, ffn2_w_gate,
           ffn2_w_up, ffn2_w_down, ffn2_norm_post, ple_norm_pre, ple_w_gate, ple_w_proj, ple_norm_post):
    b, t, d = x.shape
    n = b * t
    x1 = _ffn(x.reshape(n, d), _row(ffn1_norm_pre), ffn1_w_gate.astype(BF16), ffn1_w_up.astype(BF16),
              ffn1_w_down.astype(BF16), _row(ffn1_norm_post)).reshape(b, t, d)

    n_main = 4 * GDN_W
    n_bg = 2 * GDN_HEADS
    win_gdn = jnp.concatenate([w_in[:, :n_main + n_bg], jnp.zeros((d, V7X_LANES - n_bg), w_in.dtype)],
                              axis=1).astype(BF16)
    win_rw = w_in[:, n_main + n_bg:].astype(BF16)
    pad_bg = lambda vec: jnp.concatenate([jnp.zeros((GDN_HEADS,), F32), vec.astype(F32),
                                          jnp.zeros((V7X_LANES - n_bg,), F32)]).reshape(1, V7X_LANES)
    zero_l = jnp.zeros((RWKV_N, RWKV_W), F32)
    w2a2 = jnp.concatenate([jnp.concatenate([rwkv_w2.astype(F32), zero_l], axis=1),
                            jnp.concatenate([zero_l, rwkv_a2.astype(F32)], axis=1)], axis=0).astype(BF16)
    gmix = _row(mix_norm_pre)
    y_gdn = _gdn(x1, gmix, win_gdn, gdn_conv_w.astype(F32), pad_bg(gdn_a_log), pad_bg(gdn_dt_bias), _row(gdn_norm_w))
    y_rwkv = _rwkv(x1, gmix, win_rw, _row(rwkv_mu), _row(rwkv_w0), _row(rwkv_a0), _row(rwkv_k_k), _row(rwkv_k_a),
                   w2a2, rwkv_g2.astype(BF16), _row(rwkv_r_k), _row(rwkv_gn_w), _row(rwkv_gn_b))
    out = _tail(x1.reshape(n, d), y_gdn.reshape(n, GDN_W), y_rwkv.reshape(n, RWKV_W), p.reshape(n, D_PLE),
                w_out.astype(BF16), _row(mix_norm_post), _row(ffn2_norm_pre), ffn2_w_gate.astype(BF16),
                ffn2_w_up.astype(BF16), ffn2_w_down.astype(BF16), _row(ffn2_norm_post), _row(ple_norm_pre),
                ple_w_gate.astype(BF16), ple_w_proj.astype(BF16), _row(ple_norm_post))
    return out.reshape(b, t, d)


def kernel(x, p, ffn1_norm_pre, ffn1_w_gate, ffn1_w_up, ffn1_w_down, ffn1_norm_post, mix_norm_pre, w_in, gdn_conv_w, gdn_a_log, gdn_dt_bias, gdn_norm_w, rwkv_mu, rwkv_w0, rwkv_w2, rwkv_a0, rwkv_a2, rwkv_g2, rwkv_k_k, rwkv_k_a, rwkv_r_k, rwkv_gn_w, rwkv_gn_b, w_out, mix_norm_post, ffn2_norm_pre, ffn2_w_gate, ffn2_w_up, ffn2_w_down, ffn2_norm_post, ple_norm_pre, ple_w_gate, ple_w_proj, ple_norm_post):
    return _layer(x, p[0], ffn1_norm_pre[0], ffn1_w_gate[0], ffn1_w_up[0], ffn1_w_down[0], ffn1_norm_post[0],
                  mix_norm_pre[0], w_in[0], gdn_conv_w[0], gdn_a_log[0], gdn_dt_bias[0], gdn_norm_w[0], rwkv_mu[0],
                  rwkv_w0[0], rwkv_w2[0], rwkv_a0[0], rwkv_a2[0], rwkv_g2[0], rwkv_k_k[0], rwkv_k_a[0], rwkv_r_k[0],
                  rwkv_gn_w[0], rwkv_gn_b[0], w_out[0], mix_norm_post[0], ffn2_norm_pre[0], ffn2_w_gate[0],
                  ffn2_w_up[0], ffn2_w_down[0], ffn2_norm_post[0], ple_norm_pre[0], ple_w_gate[0], ple_w_proj[0],
                  ple_norm_post[0])
```

```python
import functools

import jax
import jax.numpy as jnp
from jax import lax
from jax.experimental import pallas as pl
from jax.experimental.pallas import tpu as pltpu

F32 = jnp.float32
BF16 = jnp.bfloat16

D_MODEL = 1024
D_FF = 2816
D_PLE = 256
GDN_HEADS = 4
GDN_D = 128
GDN_CONV = 4
RWKV_HEADS = 8
RWKV_N = 64
RWKV_W = RWKV_HEADS * RWKV_N
GDN_W = GDN_HEADS * GDN_D
LORA_WA = 128
LORA_G = 128
NORM_EPS = 1e-6
GN_EPS = 64e-5
L2_EPS = 1e-6

V7X_LANES = 128

GCOL_Z = 3 * GDN_W
GCOL_BG = 4 * GDN_W
GDN_COLS = GCOL_BG + V7X_LANES
RW_COLS = 3 * RWKV_W + LORA_WA + LORA_G

GDN_CHUNK = 128
RWKV_CHUNK = 64
HALO = 8
GDN_TILE = 512
RWKV_TILE = 256

FF_CHUNKS = ((0, 768), (768, 768), (1536, 768), (2304, 512))
ROW_GROUPS = 2
VMEM_LIMIT = 56 * 1024 * 1024


def _dot(a, b):
    return jnp.dot(a.astype(BF16), b.astype(BF16), preferred_element_type=F32)


def _dot_nt(a, b):
    return lax.dot_general(a.astype(BF16), b.astype(BF16), (((1,), (1,)), ((), ())),
                           preferred_element_type=F32)


def _rms(x, g):
    return x * lax.rsqrt(jnp.mean(x * x, axis=-1, keepdims=True) + NORM_EPS) * g


def _sigmoid(x):
    return 1.0 / (1.0 + jnp.exp(-x))


def _silu(x):
    h = 0.5 * x
    return h + h * jnp.tanh(h)


def _softplus(x):
    return jnp.maximum(x, 0.0) + jnp.log(1.0 + jnp.exp(-jnp.abs(x)))


def _split3(x):
    hi = x.astype(BF16)
    r1 = x - hi.astype(F32)
    mid = r1.astype(BF16)
    lo = (r1 - mid.astype(F32)).astype(BF16)
    return hi, mid, lo


def _cumsum_rows(tri, x):
    hi, mid, lo = _split3(x)
    return (jnp.dot(tri, hi, preferred_element_type=F32)
            + jnp.dot(tri, mid, preferred_element_type=F32)
            + jnp.dot(tri, lo, preferred_element_type=F32))


def _odd_blocks(x, b):
    return jnp.concatenate([x[s:s + b] for s in range(b, x.shape[0], 2 * b)], axis=0)


def _with_odd_blocks(base, odd, b):
    parts = []
    for j, s in enumerate(range(0, base.shape[0], 2 * b)):
        parts.append(base[s:s + b])
        parts.append(odd[j * b:(j + 1) * b])
    return jnp.concatenate(parts, axis=0)


def _unit_lower_inverses(lmats, top):
    n = lmats[0].shape[0]
    row = lax.broadcasted_iota(jnp.int32, (n, n), 0)
    col = lax.broadcasted_iota(jnp.int32, (n, n), 1)
    lower = row > col
    eye = jnp.where(row == col, 1.0, 0.0)
    xs = [eye - jnp.where(lower & ((row // 2) == (col // 2)), lm, 0.0) for lm in lmats]
    b = 2
    while b < top:
        sub = lower & ((row // (2 * b)) == (col // (2 * b))) & ((row // b) != (col // b))
        cs = [jnp.where(sub, lm, 0.0) for lm in lmats]
        if b % 8:
            ys = [_dot(c, x) for c, x in zip(cs, xs)]
            yield
            xs = [x - _dot(x, y) for x, y in zip(xs, ys)]
            yield
        else:
            zero = jnp.zeros((n, n), F32)
            ys = [_with_odd_blocks(zero, _dot(_odd_blocks(c, b), x), b) for c, x in zip(cs, xs)]
            yield
            xs = [_with_odd_blocks(x, _odd_blocks(x, b) - _dot(_odd_blocks(x, b), y), b) for x, y in zip(xs, ys)]
            yield
        b *= 2
    return xs


def _interleave(main, side):
    side_live = True
    while True:
        if side_live:
            try:
                next(side)
            except StopIteration:
                side_live = False
        try:
            next(main)
        except StopIteration:
            break
    if side_live:
        for _ in side:
            pass


def _const_spec(shape):
    return pl.BlockSpec(shape, lambda *_: (0,) * len(shape), pipeline_mode=pl.Buffered(1))


def _project(x_ref, gmix_ref, win_ref, pbuf):
    tile = x_ref.shape[0]
    h = _rms(x_ref[...], gmix_ref[...]).astype(BF16)
    pbuf[HALO:HALO + tile, :] = jnp.dot(h, win_ref[...], preferred_element_type=F32)


def _tile_clock(n_t):
    s = pl.program_id(0)
    cur = lax.rem(s, jnp.int32(2))
    t_cur = lax.rem(s, jnp.int32(n_t))
    t_prev = lax.rem(s + jnp.int32(n_t - 1), jnp.int32(n_t))
    return s == 0, t_cur == 0, t_prev == 0, cur, 1 - cur


def _ffn_stages(x_ref, rows, gpre_ref, wg_ref, wu_ref, wd_ref, gpost_ref, o_ref, a_scr):
    x = x_ref[rows, :]
    h = _rms(x, gpre_ref[...]).astype(BF16)
    yield
    for s, w in FF_CHUNKS:
        g = jnp.dot(h, wg_ref[:, s:s + w], preferred_element_type=F32)
        u = jnp.dot(h, wu_ref[:, s:s + w], preferred_element_type=F32)
        a_scr[rows, s:s + w] = (_silu(g) * u).astype(BF16)
        yield
    f = jnp.dot(a_scr[rows, :], wd_ref[...], preferred_element_type=F32)
    yield
    o_ref[rows, :] = x + 0.5 * _rms(f, gpost_ref[...])


def _staggered(chains):
    live = []
    waiting = list(chains)
    while waiting or live:
        if waiting:
            live.insert(0, waiting.pop(0))
        for gen in list(live):
            try:
                next(gen)
            except StopIteration:
                live.remove(gen)


def _ffn_kernel(x_ref, gpre_ref, wg_ref, wu_ref, wd_ref, gpost_ref, o_ref, a_scr):
    part = x_ref.shape[0] // ROW_GROUPS
    _staggered([_ffn_stages(x_ref, slice(i * part, (i + 1) * part), gpre_ref, wg_ref, wu_ref, wd_ref, gpost_ref,
                            o_ref, a_scr) for i in range(ROW_GROUPS)])


def _ffn(x2d, gpre, wg, wu, wd, gpost, tm=512):
    n = x2d.shape[0]
    row = pl.BlockSpec((tm, D_MODEL), lambda i: (i, 0))
    return pl.pallas_call(
        _ffn_kernel,
        out_shape=jax.ShapeDtypeStruct(x2d.shape, F32),
        grid=(n // tm,),
        in_specs=[row, _const_spec((1, D_MODEL)), _const_spec((D_MODEL, D_FF)), _const_spec((D_MODEL, D_FF)),
                  _const_spec((D_FF, D_MODEL)), _const_spec((1, D_MODEL))],
        out_specs=row,
        scratch_shapes=[pltpu.VMEM((tm, D_FF), BF16)],
        compiler_params=pltpu.CompilerParams(dimension_semantics=("arbitrary",), vmem_limit_bytes=VMEM_LIMIT),
        name="ffn",
    )(x2d, gpre, wg, wu, wd, gpost)


def _gdn_kernel(x_ref, gmix_ref, win_ref, convw_ref, alog_ref, dtb_ref, nw_ref, y_ref,
                pbuf, q_s, k_s, v_s, bg_s, s_ref, z_scr, zl_scr, u_scr, aqk_scr, kdt_scr, egl_scr, *, n_t):
    tt = x_ref.shape[0]
    c = GDN_CHUNK
    nh = GDN_HEADS
    nc = tt // c
    first_step, seq_start, prev_seq_start, cur, prev = _tile_clock(n_t)

    @pl.when(first_step)
    def _():
        for ref in (z_scr, zl_scr, u_scr, aqk_scr, kdt_scr, egl_scr):
            ref[1] = jnp.zeros(ref.shape[1:], ref.dtype)

    @pl.when(first_step | prev_seq_start)
    def _():
        s_ref[...] = jnp.zeros(s_ref.shape, F32)

    @pl.when(seq_start)
    def _():
        pbuf[0:HALO, :] = jnp.zeros((HALO, pbuf.shape[1]), F32)

    row = lax.broadcasted_iota(jnp.int32, (c, c), 0)
    col = lax.broadcasted_iota(jnp.int32, (c, c), 1)
    incl = row >= col
    strict = row > col
    tri = jnp.where(incl, 1.0, 0.0).astype(BF16)

    def prepare():
        _project(x_ref, gmix_ref, win_ref, pbuf)
        yield
        for grp, dst in ((0, q_s), (1, k_s), (2, v_s)):
            for hd in range(nh):
                c0 = grp * GDN_W + hd * GDN_D
                xe = pbuf[0:HALO + tt, c0:c0 + GDN_D]
                acc = xe * convw_ref[0:1, c0:c0 + GDN_D]
                for j in range(1, GDN_CONV):
                    acc = pltpu.roll(acc, 1, axis=0) + xe * convw_ref[j:j + 1, c0:c0 + GDN_D]
                y = _silu(acc[HALO:])
                if grp < 2:
                    y = y * lax.rsqrt(jnp.sum(y * y, axis=-1, keepdims=True) + L2_EPS)
                if grp == 0:
                    y = y * (GDN_D ** -0.5)
                dst[:, hd * GDN_D:(hd + 1) * GDN_D] = y
        lane = lax.broadcasted_iota(jnp.int32, (tt, V7X_LANES), 1)
        pb = pbuf[HALO:HALO + tt, GCOL_BG:GCOL_BG + V7X_LANES]
        gdec = -jnp.exp(alog_ref[...]) * _softplus(pb + dtb_ref[...])
        bg_s[...] = jnp.where(lane < nh, _sigmoid(pb), jnp.where(lane < 2 * nh, gdec, 0.0))
        z_scr[cur] = pbuf[HALO:HALO + tt, GCOL_Z:GCOL_Z + GDN_W]
        pbuf[0:HALO, :] = pbuf[tt:tt + HALO, :]

        units = []
        for ci in range(nc):
            rows = slice(ci * c, (ci + 1) * c)
            bg = bg_s[rows, :]
            gc = _cumsum_rows(tri, bg)
            gct = gc.T
            for hd in range(nh):
                sl = slice(hd * GDN_D, (hd + 1) * GDN_D)
                gcol = gc[:, nh + hd:nh + hd + 1]
                grow = gct[nh + hd:nh + hd + 1, :]
                glast = gc[c - 1:c, nh + hd:nh + hd + 1]
                beta = bg[:, hd:hd + 1]
                k = k_s[rows, sl]
                units.append(dict(
                    ci=ci, hd=hd, gcol=gcol, glast=glast, k=k, q=q_s[rows, sl], kb=k * beta,
                    vb=v_s[rows, sl] * beta, eg=jnp.exp(gcol),
                    dec=jnp.where(incl, jnp.exp(jnp.where(incl, gcol - grow, 0.0)), 0.0)))
        yield
        kqs = [_dot_nt(jnp.concatenate([un["kb"], un["q"]], axis=0), un["k"]) for un in units]
        yield
        tinvs = yield from _unit_lower_inverses(
            [jnp.where(strict, kq[:c] * un["dec"], 0.0) for kq, un in zip(kqs, units)], c)
        uws = [_dot(tinv, jnp.concatenate([un["vb"], un["kb"] * un["eg"]], axis=1))
               for tinv, un in zip(tinvs, units)]
        yield
        for un, kq, uw in zip(units, kqs, uws):
            ci, hd = un["ci"], un["hd"]
            zl_scr[cur, ci, hd] = jnp.concatenate([uw[:, GDN_D:], un["q"] * un["eg"]], axis=0).astype(BF16)
            u_scr[cur, ci, hd] = uw[:, :GDN_D]
            aqk_scr[cur, ci, hd] = (kq[c:] * un["dec"]).astype(BF16)
            kdt_scr[cur, ci, hd] = (un["k"] * jnp.exp(un["glast"] - un["gcol"])).T.astype(BF16)
            egl_scr[cur, ci, hd] = jnp.broadcast_to(jnp.exp(un["glast"]), (8, GDN_D))

    def finish():
        for ci in range(nc):
            rows = slice(ci * c, (ci + 1) * c)
            zss = [_dot(zl_scr[prev, ci, hd], s_ref[hd]) for hd in range(nh)]
            yield
            v_news = [u_scr[prev, ci, hd] - zss[hd][:c] for hd in range(nh)]
            outs = [zss[hd][c:] + _dot(aqk_scr[prev, ci, hd], v_news[hd]) for hd in range(nh)]
            upds = [_dot(kdt_scr[prev, ci, hd], v_news[hd]) for hd in range(nh)]
            for hd in range(nh):
                s_ref[hd] = s_ref[hd] * egl_scr[prev, ci, hd][0:1, :] + upds[hd]
            for hd in range(nh):
                sl = slice(hd * GDN_D, (hd + 1) * GDN_D)
                o = outs[hd]
                o = o * lax.rsqrt(jnp.mean(o * o, axis=-1, keepdims=True) + NORM_EPS) * nw_ref[...]
                y_ref[rows, sl] = (o * _silu(z_scr[prev, rows, sl])).astype(y_ref.dtype)
            yield

    _interleave(prepare(), finish())


def _flat_tile_specs(n_t, n_tiles, tile, width_in, width_out):
    def in_map(s):
        i = jnp.minimum(s, n_tiles - 1)
        return (i // n_t, i % n_t, 0)

    def out_map(s):
        i = jnp.maximum(s - 1, 0)
        return (i // n_t, i % n_t, 0)

    return pl.BlockSpec((None, tile, width_in), in_map), pl.BlockSpec((None, tile, width_out), out_map)


def _gdn(x1, gmix, win, convw, alog, dtb, nw):
    b, t, _ = x1.shape
    tt = GDN_TILE
    nc = tt // GDN_CHUNK
    n_t = t // tt
    consts = (gmix, win, convw, alog, dtb, nw)
    in_spec, out_spec = _flat_tile_specs(n_t, b * n_t, tt, D_MODEL, GDN_W)
    return pl.pallas_call(
        functools.partial(_gdn_kernel, n_t=n_t),
        out_shape=jax.ShapeDtypeStruct((b, t, GDN_W), BF16),
        grid=(b * n_t + 1,),
        in_specs=[in_spec] + [_const_spec(c.shape) for c in consts],
        out_specs=out_spec,
        scratch_shapes=[pltpu.VMEM((tt + HALO, GDN_COLS), F32),
                        pltpu.VMEM((tt, GDN_W), F32), pltpu.VMEM((tt, GDN_W), F32), pltpu.VMEM((tt, GDN_W), F32),
                        pltpu.VMEM((tt, V7X_LANES), F32),
                        pltpu.VMEM((GDN_HEADS, GDN_D, GDN_D), F32),
                        pltpu.VMEM((2, tt, GDN_W), F32),
                        pltpu.VMEM((2, nc, GDN_HEADS, 2 * GDN_CHUNK, GDN_D), BF16),
                        pltpu.VMEM((2, nc, GDN_HEADS, GDN_CHUNK, GDN_D), F32),
                        pltpu.VMEM((2, nc, GDN_HEADS, GDN_CHUNK, GDN_CHUNK), BF16),
                        pltpu.VMEM((2, nc, GDN_HEADS, GDN_D, GDN_CHUNK), BF16),
                        pltpu.VMEM((2, nc, GDN_HEADS, 8, GDN_D), F32)],
        compiler_params=pltpu.CompilerParams(dimension_semantics=("arbitrary",), vmem_limit_bytes=VMEM_LIMIT),
        name="gdn",
    )(x1, *consts)


def _rwkv_kernel(x_ref, gmix_ref, win_ref, mu_ref, w0_ref, a0_ref, kkw_ref, kaw_ref, w2a2_ref, g2_ref,
                 rk_ref, gnw_ref, gnb_ref, y_ref,
                 pbuf, lw_s, kk_s, a_s, r_s, k_s, v_s, gate_s,
                 s_ref, zl_scr, ut_scr, yv_scr, arb_scr, kbt_scr, vb_scr, dm_scr, *, n_t):
    tt = x_ref.shape[0]
    c = RWKV_CHUNK
    n = RWKV_N
    npair = RWKV_HEADS // 2
    nc = tt // c
    first_step, seq_start, prev_seq_start, cur, prev = _tile_clock(n_t)

    @pl.when(first_step)
    def _():
        for ref in (r_s, k_s, v_s, gate_s, zl_scr, ut_scr, yv_scr, arb_scr, kbt_scr, vb_scr, dm_scr):
            ref[1] = jnp.zeros(ref.shape[1:], ref.dtype)

    @pl.when(first_step | prev_seq_start)
    def _():
        s_ref[...] = jnp.zeros(s_ref.shape, F32)

    @pl.when(seq_start)
    def _():
        pbuf[0:HALO, :] = jnp.zeros((HALO, pbuf.shape[1]), F32)

    lane_t = lax.broadcasted_iota(jnp.int32, (tt, V7X_LANES), 1)
    lo_half = lane_t < n

    def shifted(c0, width):
        now = pbuf[HALO:HALO + tt, c0:c0 + width]
        before = pbuf[HALO - 1:HALO - 1 + tt, c0:c0 + width]
        return now + (before - now) * mu_ref[:, c0:c0 + width]

    def pointwise():
        lwa = shifted(3 * RWKV_W, LORA_WA)
        lwa = jnp.where(lo_half, jnp.tanh(lwa), lwa)
        wa = jnp.dot(lwa.astype(BF16), w2a2_ref[...], preferred_element_type=F32)
        lg = shifted(3 * RWKV_W + LORA_WA, LORA_G)
        gate_s[cur] = jnp.dot(_sigmoid(lg).astype(BF16), g2_ref[...], preferred_element_type=F32)
        for p in range(npair):
            sl = slice(p * V7X_LANES, (p + 1) * V7X_LANES)
            w_log = -_softplus(-(w0_ref[:, sl] + wa[:, sl])) - 0.5
            lw_s[:, sl] = -jnp.exp(w_log)
            a = _sigmoid(a0_ref[:, sl] + wa[:, RWKV_W + p * V7X_LANES:RWKV_W + (p + 1) * V7X_LANES])
            a_s[:, sl] = a
            r_s[cur, :, sl] = shifted(p * V7X_LANES, V7X_LANES)
            kr = shifted(RWKV_W + p * V7X_LANES, V7X_LANES)
            v_s[cur, :, sl] = shifted(2 * RWKV_W + p * V7X_LANES, V7X_LANES)
            kx = kr * kkw_ref[:, sl]
            sq = kx * kx
            s_lo = jnp.sum(jnp.where(lo_half, sq, 0.0), axis=-1, keepdims=True)
            s_hi = jnp.sum(jnp.where(lo_half, 0.0, sq), axis=-1, keepdims=True)
            kk_s[:, sl] = kx * lax.rsqrt(jnp.where(lo_half, s_lo, s_hi) + L2_EPS)
            k_s[cur, :, sl] = kr * (1.0 + (a - 1.0) * kaw_ref[:, sl])
        pbuf[0:HALO, :] = pbuf[tt:tt + HALO, :]

    row_c = lax.broadcasted_iota(jnp.int32, (c, c), 0)
    col_c = lax.broadcasted_iota(jnp.int32, (c, c), 1)
    tri = jnp.where(row_c >= col_c, 1.0, 0.0).astype(BF16)
    row = lax.broadcasted_iota(jnp.int32, (c, 2 * n), 0)
    lane = lax.broadcasted_iota(jnp.int32, (c, 2 * n), 1)
    m0 = lane < n
    jj = jnp.where(m0, lane, lane - n)
    strict = row > jj
    incl = row >= jj
    row2 = lax.broadcasted_iota(jnp.int32, (2 * n, 2 * n), 0)
    lane2 = lax.broadcasted_iota(jnp.int32, (2 * n, 2 * n), 1)
    same_head = (row2 < n) == (lane2 < n)

    def sel0(x):
        return jnp.where(m0, x, 0.0)

    def sel1(x):
        return jnp.where(m0, 0.0, x)

    def head_sum(t):
        s_lo = jnp.sum(sel0(t), axis=-1, keepdims=True)
        s_hi = jnp.sum(sel1(t), axis=-1, keepdims=True)
        return jnp.where(m0, s_lo, s_hi)

    def prepare():
        _project(x_ref, gmix_ref, win_ref, pbuf)
        yield
        pointwise()
        units = []
        for ci in range(nc):
            rows = slice(ci * c, (ci + 1) * c)
            g_all = _cumsum_rows(tri, lw_s[rows, :])
            for p in range(npair):
                sl = slice(p * 2 * n, (p + 1) * 2 * n)
                g = g_all[:, sl]
                gmid = g[c // 2 - 1:c // 2, :]
                gend = g[c - 1:c, :]
                e_inv = jnp.exp(gmid - g)
                e_end = jnp.exp(gend - g)
                egm = jnp.exp(gmid)
                k = k_s[cur, rows, sl]
                kk = kk_s[rows, sl]
                b = kk * a_s[rows, sl]
                kap_t = kk * jnp.exp(g - lw_s[rows, sl] - gmid)
                r_t = r_s[cur, rows, sl] * jnp.exp(g - gmid)
                units.append(dict(ci=ci, p=p, v=v_s[cur, rows, sl], kap_t=kap_t, r_t=r_t, b_t=b * e_inv,
                                  k_t=k * e_inv, kb=jnp.concatenate([k * e_end, -(b * e_end)], axis=0), egm=egm,
                                  egend=jnp.exp(gend)))
        yield
        a0s = [_dot_nt(jnp.concatenate([sel0(un["kap_t"]), sel0(un["r_t"])], axis=0),
                       jnp.concatenate([un["b_t"], un["k_t"]], axis=0)) for un in units]
        yield
        a1s = [_dot_nt(jnp.concatenate([sel1(un["kap_t"]), sel1(un["r_t"])], axis=0),
                       jnp.concatenate([un["k_t"], un["b_t"]], axis=0)) for un in units]
        yield
        lmats = [jnp.concatenate([jnp.where(strict & m0, a0[:c], 0.0), jnp.where(strict & (~m0), a1[:c], 0.0)],
                                 axis=0) for a0, a1 in zip(a0s, a1s)]
        avys = []
        for un, a0, a1 in zip(units, a0s, a1s):
            a_uk = jnp.where(strict, jnp.where(m0, a1[:c], a0[:c]), 0.0)
            a_rk = jnp.where(incl, jnp.where(m0, a1[c:], a0[c:]), 0.0)
            v_sw = jnp.concatenate([sel1(un["v"]), sel0(un["v"])], axis=0)
            avys.append(_dot(jnp.concatenate([a_uk, a_rk], axis=0), v_sw))
        yield
        tbds = yield from _unit_lower_inverses(lmats, c)
        wus = []
        for un, tbd, avy in zip(units, tbds, avys):
            tcat = tbd[:c] + tbd[c:]
            kap_b = un["kap_t"] * un["egm"]
            av = avy[:c]
            x = jnp.concatenate([jnp.concatenate([sel0(kap_b), sel0(av)], axis=1),
                                 jnp.concatenate([sel1(kap_b), sel1(av)], axis=1)], axis=0)
            wus.append(_dot(tcat, x))
        yield
        for un, a0, a1, avy, wu in zip(units, a0s, a1s, avys, wus):
            ci, p = un["ci"], un["p"]
            zl_scr[cur, ci, p] = jnp.concatenate([wu[:, :2 * n], un["r_t"] * un["egm"]], axis=0).astype(BF16)
            ut_scr[cur, ci, p] = wu[:, 2 * n:]
            yv_scr[cur, ci, p] = avy[c:]
            arb_scr[cur, ci, p] = jnp.where(incl, jnp.where(m0, a0[c:], a1[c:]), 0.0).astype(BF16)
            kbt_scr[cur, ci, p] = un["kb"].T.astype(BF16)
            vb_scr[cur, ci, p] = un["v"].astype(BF16)
            dm_scr[cur, ci, p] = jnp.broadcast_to(un["egend"], (2 * n, 2 * n)).T

    def finish():
        for ci in range(nc):
            rows = slice(ci * c, (ci + 1) * c)
            zss = [_dot(zl_scr[prev, ci, p], s_ref[p]) for p in range(npair)]
            yield
            us = [zss[p][:c] + ut_scr[prev, ci, p] for p in range(npair)]
            ys = [zss[p][c:] - _dot(arb_scr[prev, ci, p], jnp.concatenate([sel0(us[p]), sel1(us[p])], axis=0))
                  + yv_scr[prev, ci, p] for p in range(npair)]
            upds = [jnp.dot(kbt_scr[prev, ci, p], jnp.concatenate([vb_scr[prev, ci, p], us[p].astype(BF16)], axis=0),
                            preferred_element_type=F32) for p in range(npair)]
            for p in range(npair):
                s_ref[p] = s_ref[p] * dm_scr[prev, ci, p] + jnp.where(same_head, upds[p], 0.0)
            for p in range(npair):
                sl = slice(p * 2 * n, (p + 1) * 2 * n)
                y = ys[p]
                mean = head_sum(y) * (1.0 / n)
                dlt = y - mean
                var = head_sum(dlt * dlt) * (1.0 / n)
                yn = dlt * lax.rsqrt(var + GN_EPS) * gnw_ref[:, sl] + gnb_ref[:, sl]
                bonus = head_sum(r_s[prev, rows, sl] * k_s[prev, rows, sl] * rk_ref[:, sl]) * v_s[prev, rows, sl]
                y_ref[rows, sl] = ((yn + bonus) * gate_s[prev, rows, sl]).astype(y_ref.dtype)
            yield

    _interleave(prepare(), finish())


def _rwkv(x1, gmix, win, mu, w0, a0, k_k, k_a, w2a2, g2, r_k, gn_w, gn_b):
    b, t, _ = x1.shape
    tt = RWKV_TILE
    nc = tt // RWKV_CHUNK
    n_t = t // tt
    npair = RWKV_HEADS // 2
    slab = 2 * RWKV_N
    consts = (gmix, win, mu, w0, a0, k_k, k_a, w2a2, g2, r_k, gn_w, gn_b)
    wide = pltpu.VMEM((tt, RWKV_W), F32)
    wide2 = pltpu.VMEM((2, tt, RWKV_W), F32)
    in_spec, out_spec = _flat_tile_specs(n_t, b * n_t, tt, D_MODEL, RWKV_W)
    return pl.pallas_call(
        functools.partial(_rwkv_kernel, n_t=n_t),
        out_shape=jax.ShapeDtypeStruct((b, t, RWKV_W), BF16),
        grid=(b * n_t + 1,),
        in_specs=[in_spec] + [_const_spec(c.shape) for c in consts],
        out_specs=out_spec,
        scratch_shapes=[pltpu.VMEM((tt + HALO, RW_COLS), F32), wide, wide, wide, wide2, wide2, wide2, wide2,
                        pltpu.VMEM((npair, slab, slab), F32),
                        pltpu.VMEM((2, nc, npair, 2 * RWKV_CHUNK, slab), BF16),
                        pltpu.VMEM((2, nc, npair, RWKV_CHUNK, slab), F32),
                        pltpu.VMEM((2, nc, npair, RWKV_CHUNK, slab), F32),
                        pltpu.VMEM((2, nc, npair, RWKV_CHUNK, slab), BF16),
                        pltpu.VMEM((2, nc, npair, slab, 2 * RWKV_CHUNK), BF16),
                        pltpu.VMEM((2, nc, npair, RWKV_CHUNK, slab), BF16),
                        pltpu.VMEM((2, nc, npair, slab, slab), F32)],
        compiler_params=pltpu.CompilerParams(dimension_semantics=("arbitrary",), vmem_limit_bytes=VMEM_LIMIT),
        name="rwkv",
    )(x1, *consts)


def _tail_stages(rows, x_ref, yg_ref, yr_ref, p_ref, wout_ref, gmp_ref, gpre_ref, wg_ref, wu_ref, wd_ref, gpost_ref,
                 gple_ref, wpg_ref, wpp_ref, gplep_ref, o_ref, a_scr):
    m = (jnp.dot(yg_ref[rows, :], wout_ref[0:GDN_W, :], preferred_element_type=F32)
         + jnp.dot(yr_ref[rows, :], wout_ref[GDN_W:GDN_W + RWKV_W, :], preferred_element_type=F32))
    yield
    x = x_ref[rows, :] + _rms(m, gmp_ref[...])
    h = _rms(x, gpre_ref[...]).astype(BF16)
    yield
    for s, w in FF_CHUNKS:
        g = jnp.dot(h, wg_ref[:, s:s + w], preferred_element_type=F32)
        u = jnp.dot(h, wu_ref[:, s:s + w], preferred_element_type=F32)
        a_scr[rows, s:s + w] = (_silu(g) * u).astype(BF16)
        yield
    f = jnp.dot(a_scr[rows, :], wd_ref[...], preferred_element_type=F32)
    yield
    x = x + 0.5 * _rms(f, gpost_ref[...])
    gate = _sigmoid(jnp.dot(_rms(x, gple_ref[...]).astype(BF16), wpg_ref[...], preferred_element_type=F32))
    emb = jnp.dot(p_ref[rows, :].astype(BF16), wpp_ref[...], preferred_element_type=F32)
    yield
    o_ref[rows, :] = x + _rms(gate * emb, gplep_ref[...])


def _tail_kernel(x_ref, *refs):
    part = x_ref.shape[0] // ROW_GROUPS
    _staggered([_tail_stages(slice(i * part, (i + 1) * part), x_ref, *refs) for i in range(ROW_GROUPS)])


def _tail(x1, yg, yr, p, wout, gmp, gpre, wg, wu, wd, gpost, gple, wpg, wpp, gplep, tm=512):
    n = x1.shape[0]
    row = pl.BlockSpec((tm, D_MODEL), lambda i: (i, 0))
    half = pl.BlockSpec((tm, GDN_W), lambda i: (i, 0))
    ple = pl.BlockSpec((tm, D_PLE), lambda i: (i, 0))
    consts = (wout, gmp, gpre, wg, wu, wd, gpost, gple, wpg, wpp, gplep)
    return pl.pallas_call(
        _tail_kernel,
        out_shape=jax.ShapeDtypeStruct(x1.shape, F32),
        grid=(n // tm,),
        in_specs=[row, half, half, ple] + [_const_spec(c.shape) for c in consts],
        out_specs=row,
        scratch_shapes=[pltpu.VMEM((tm, D_FF), BF16)],
        compiler_params=pltpu.CompilerParams(dimension_semantics=("arbitrary",), vmem_limit_bytes=VMEM_LIMIT),
        name="tail",
    )(x1, yg, yr, p, *consts)


def _row(v):
    return v.reshape(1, -1).astype(F32)


def _layer(x, p, ffn1_norm_pre, ffn1_w_gate, ffn1_w_up, ffn1_w_down, ffn1_norm_post, mix_norm_pre, w_in,
           gdn_conv_w, gdn_a_log, gdn_dt_bias, gdn_norm_w, rwkv_mu, rwkv_w0, rwkv_w2, rwkv_a0, rwkv_a2, rwkv_g2,
           rwkv_k_k, rwkv_k_a, rwkv_r_k, rwkv_gn_w, rwkv_gn_b, w_out, mix_norm_post, ffn2_norm_pre, ffn2_w_gate,
           ffn2_w_up, ffn2_w_down, ffn2_norm_post, ple_norm_pre, ple_w_gate, ple_w_proj, ple_norm_post):
    b, t, d = x.shape
    n = b * t
    x1 = _ffn(x.reshape(n, d), _row(ffn1_norm_pre), ffn1_w_gate.astype(BF16), ffn1_w_up.astype(BF16),
              ffn1_w_down.astype(BF16), _row(ffn1_norm_post)).reshape(b, t, d)

    n_main = 4 * GDN_W
    n_bg = 2 * GDN_HEADS
    win_gdn = jnp.concatenate([w_in[:, :n_main + n_bg], jnp.zeros((d, V7X_LANES - n_bg), w_in.dtype)],
                              axis=1).astype(BF16)
    win_rw = w_in[:, n_main + n_bg:].astype(BF16)
    pad_bg = lambda vec: jnp.concatenate([jnp.zeros((GDN_HEADS,), F32), vec.astype(F32),
                                          jnp.zeros((V7X_LANES - n_bg,), F32)]).reshape(1, V7X_LANES)
    zero_l = jnp.zeros((RWKV_N, RWKV_W), F32)
    w2a2 = jnp.concatenate([jnp.concatenate([rwkv_w2.astype(F32), zero_l], axis=1),
                            jnp.concatenate([zero_l, rwkv_a2.astype(F32)], axis=1)], axis=0).astype(BF16)
    gmix = _row(mix_norm_pre)
    y_gdn = _gdn(x1, gmix, win_gdn, gdn_conv_w.astype(F32), pad_bg(gdn_a_log), pad_bg(gdn_dt_bias), _row(gdn_norm_w))
    y_rwkv = _rwkv(x1, gmix, win_rw, _row(rwkv_mu), _row(rwkv_w0), _row(rwkv_a0), _row(rwkv_k_k), _row(rwkv_k_a),
                   w2a2, rwkv_g2.astype(BF16), _row(rwkv_r_k), _row(rwkv_gn_w), _row(rwkv_gn_b))
    out = _tail(x1.reshape(n, d), y_gdn.reshape(n, GDN_W), y_rwkv.reshape(n, RWKV_W), p.reshape(n, D_PLE),
                w_out.astype(BF16), _row(mix_norm_post), _row(ffn2_norm_pre), ffn2_w_gate.astype(BF16),
                ffn2_w_up.astype(BF16), ffn2_w_down.astype(BF16), _row(ffn2_norm_post), _row(ple_norm_pre),
                ple_w_gate.astype(BF16), ple_w_proj.astype(BF16), _row(ple_norm_post))
    return out.reshape(b, t, d)


def kernel(x, p, ffn1_norm_pre, ffn1_w_gate, ffn1_w_up, ffn1_w_down, ffn1_norm_post, mix_norm_pre, w_in, gdn_conv_w, gdn_a_log, gdn_dt_bias, gdn_norm_w, rwkv_mu, rwkv_w0, rwkv_w2, rwkv_a0, rwkv_a2, rwkv_g2, rwkv_k_k, rwkv_k_a, rwkv_r_k, rwkv_gn_w, rwkv_gn_b, w_out, mix_norm_post, ffn2_norm_pre, ffn2_w_gate, ffn2_w_up, ffn2_w_down, ffn2_norm_post, ple_norm_pre, ple_w_gate, ple_w_proj, ple_norm_post):
    return _layer(x, p[0], ffn1_norm_pre[0], ffn1_w_gate[0], ffn1_w_up[0], ffn1_w_down[0], ffn1_norm_post[0],
                  mix_norm_pre[0], w_in[0], gdn_conv_w[0], gdn_a_log[0], gdn_dt_bias[0], gdn_norm_w[0], rwkv_mu[0],
                  rwkv_w0[0], rwkv_w2[0], rwkv_a0[0], rwkv_a2[0], rwkv_g2[0], rwkv_k_k[0], rwkv_k_a[0], rwkv_r_k[0],
                  rwkv_gn_w[0], rwkv_gn_b[0], w_out[0], mix_norm_post[0], ffn2_norm_pre[0], ffn2_w_gate[0],
                  ffn2_w_up[0], ffn2_w_down[0], ffn2_norm_post[0], ple_norm_pre[0], ple_w_gate[0], ple_w_proj[0],
                  ple_norm_post[0])
```

```python
import functools

import jax
import jax.numpy as jnp
from jax import lax
from jax.experimental import pallas as pl
from jax.experimental.pallas import tpu as pltpu

F32 = jnp.float32
BF16 = jnp.bfloat16

D_MODEL = 1024
D_FF = 2816
D_PLE = 256
GDN_HEADS = 4
GDN_D = 128
GDN_CONV = 4
RWKV_HEADS = 8
RWKV_N = 64
RWKV_W = RWKV_HEADS * RWKV_N
GDN_W = GDN_HEADS * GDN_D
LORA_WA = 128
LORA_G = 128
NORM_EPS = 1e-6
GN_EPS = 64e-5
L2_EPS = 1e-6

V7X_LANES = 128

GCOL_Z = 3 * GDN_W
GCOL_BG = 4 * GDN_W
GDN_COLS = GCOL_BG + V7X_LANES
RW_COLS = 3 * RWKV_W + LORA_WA + LORA_G

GDN_CHUNK = 128
RWKV_CHUNK = 64
HALO = 8
GDN_TILE = 512
RWKV_TILE = 256

FF_CHUNKS = ((0, 768), (768, 768), (1536, 768), (2304, 512))
ROW_GROUPS = 4
VMEM_LIMIT = 58 * 1024 * 1024


def _dot(a, b):
    return jnp.dot(a.astype(BF16), b.astype(BF16), preferred_element_type=F32)


def _dot_nt(a, b):
    return lax.dot_general(a.astype(BF16), b.astype(BF16), (((1,), (1,)), ((), ())),
                           preferred_element_type=F32)


def _rms(x, g):
    return x * lax.rsqrt(jnp.mean(x * x, axis=-1, keepdims=True) + NORM_EPS) * g


def _sigmoid(x):
    return 1.0 / (1.0 + jnp.exp(-x))


def _silu(x):
    h = 0.5 * x
    return h + h * jnp.tanh(h)


def _softplus(x):
    return jnp.maximum(x, 0.0) + jnp.log(1.0 + jnp.exp(-jnp.abs(x)))


def _split3(x):
    hi = x.astype(BF16)
    r1 = x - hi.astype(F32)
    mid = r1.astype(BF16)
    lo = (r1 - mid.astype(F32)).astype(BF16)
    return hi, mid, lo


def _cumsum_rows(tri, x):
    hi, mid, lo = _split3(x)
    return (jnp.dot(tri, hi, preferred_element_type=F32)
            + jnp.dot(tri, mid, preferred_element_type=F32)
            + jnp.dot(tri, lo, preferred_element_type=F32))


def _odd_blocks(x, b):
    return jnp.concatenate([x[s:s + b] for s in range(b, x.shape[0], 2 * b)], axis=0)


def _with_odd_blocks(base, odd, b):
    parts = []
    for j, s in enumerate(range(0, base.shape[0], 2 * b)):
        parts.append(base[s:s + b])
        parts.append(odd[j * b:(j + 1) * b])
    return jnp.concatenate(parts, axis=0)


def _unit_lower_inverses(lmats, top):
    n = lmats[0].shape[0]
    row = lax.broadcasted_iota(jnp.int32, (n, n), 0)
    col = lax.broadcasted_iota(jnp.int32, (n, n), 1)
    lower = row > col
    eye = jnp.where(row == col, 1.0, 0.0)
    xs = [eye - jnp.where(lower & ((row // 2) == (col // 2)), lm, 0.0) for lm in lmats]
    b = 2
    while b < top:
        sub = lower & ((row // (2 * b)) == (col // (2 * b))) & ((row // b) != (col // b))
        cs = [jnp.where(sub, lm, 0.0) for lm in lmats]
        if b % 8:
            ys = [_dot(c, x) for c, x in zip(cs, xs)]
            yield
            xs = [x - _dot(x, y) for x, y in zip(xs, ys)]
            yield
        else:
            zero = jnp.zeros((n, n), F32)
            ys = [_with_odd_blocks(zero, _dot(_odd_blocks(c, b), x), b) for c, x in zip(cs, xs)]
            yield
            xs = [_with_odd_blocks(x, _odd_blocks(x, b) - _dot(_odd_blocks(x, b), y), b) for x, y in zip(xs, ys)]
            yield
        b *= 2
    return xs


def _interleave(main, side):
    side_live = True
    while True:
        if side_live:
            try:
                next(side)
            except StopIteration:
                side_live = False
        try:
            next(main)
        except StopIteration:
            break
    if side_live:
        for _ in side:
            pass


def _const_spec(shape):
    return pl.BlockSpec(shape, lambda *_: (0,) * len(shape), pipeline_mode=pl.Buffered(1))


def _tile_clock(n_t):
    s = pl.program_id(0)
    cur = lax.rem(s, jnp.int32(2))
    t_cur = lax.rem(s, jnp.int32(n_t))
    t_prev = lax.rem(s + jnp.int32(n_t - 1), jnp.int32(n_t))
    return s == 0, t_cur == 0, t_prev == 0, cur, 1 - cur


def _ffn_stages(x_ref, rows, gpre_ref, wg_ref, wu_ref, wd_ref, gpost_ref, o_ref, a_scr):
    x = x_ref[rows, :]
    h = _rms(x, gpre_ref[...]).astype(BF16)
    yield
    for s, w in FF_CHUNKS:
        g = jnp.dot(h, wg_ref[:, s:s + w], preferred_element_type=F32)
        u = jnp.dot(h, wu_ref[:, s:s + w], preferred_element_type=F32)
        a_scr[rows, s:s + w] = (_silu(g) * u).astype(BF16)
        yield
    f = jnp.dot(a_scr[rows, :], wd_ref[...], preferred_element_type=F32)
    yield
    o_ref[rows, :] = x + 0.5 * _rms(f, gpost_ref[...])


def _staggered(chains):
    live = []
    waiting = list(chains)
    while waiting or live:
        if waiting:
            live.insert(0, waiting.pop(0))
        for gen in list(live):
            try:
                next(gen)
            except StopIteration:
                live.remove(gen)


def _ffn_kernel(x_ref, gpre_ref, wg_ref, wu_ref, wd_ref, gpost_ref, o_ref, a_scr):
    part = x_ref.shape[0] // ROW_GROUPS
    _staggered([_ffn_stages(x_ref, slice(i * part, (i + 1) * part), gpre_ref, wg_ref, wu_ref, wd_ref, gpost_ref,
                            o_ref, a_scr) for i in range(ROW_GROUPS)])


def _ffn(x2d, gpre, wg, wu, wd, gpost, tm=1024):
    n = x2d.shape[0]
    row = pl.BlockSpec((tm, D_MODEL), lambda i: (i, 0))
    return pl.pallas_call(
        _ffn_kernel,
        out_shape=jax.ShapeDtypeStruct(x2d.shape, F32),
        grid=(n // tm,),
        in_specs=[row, _const_spec((1, D_MODEL)), _const_spec((D_MODEL, D_FF)), _const_spec((D_MODEL, D_FF)),
                  _const_spec((D_FF, D_MODEL)), _const_spec((1, D_MODEL))],
        out_specs=row,
        scratch_shapes=[pltpu.VMEM((tm, D_FF), BF16)],
        compiler_params=pltpu.CompilerParams(dimension_semantics=("arbitrary",), vmem_limit_bytes=VMEM_LIMIT),
        name="ffn",
    )(x2d, gpre, wg, wu, wd, gpost)


def _gdn_kernel(x_ref, gmix_ref, win_ref, convw_ref, alog_ref, dtb_ref, nw_ref, y_ref,
                pbuf, q_s, k_s, v_s, bg_s, s_ref, z_scr, zl_scr, u_scr, aqk_scr, kdt_scr, egl_scr, *, n_t):
    tt = x_ref.shape[0]
    c = GDN_CHUNK
    nh = GDN_HEADS
    nc = tt // c
    first_step, seq_start, prev_seq_start, cur, prev = _tile_clock(n_t)

    @pl.when(first_step)
    def _():
        for ref in (z_scr, zl_scr, u_scr, aqk_scr, kdt_scr, egl_scr):
            ref[1] = jnp.zeros(ref.shape[1:], ref.dtype)

    @pl.when(first_step | prev_seq_start)
    def _():
        s_ref[...] = jnp.zeros(s_ref.shape, F32)

    @pl.when(seq_start)
    def _():
        pbuf[0:HALO, :] = jnp.zeros((HALO, pbuf.shape[1]), F32)

    row = lax.broadcasted_iota(jnp.int32, (c, c), 0)
    col = lax.broadcasted_iota(jnp.int32, (c, c), 1)
    incl = row >= col
    strict = row > col
    tri = jnp.where(incl, 1.0, 0.0).astype(BF16)

    def prepare():
        h = _rms(x_ref[...], gmix_ref[...]).astype(BF16)
        z_scr[cur] = jnp.dot(h, win_ref[:, GCOL_Z:GCOL_Z + GDN_W], preferred_element_type=F32)
        pb = jnp.dot(h, win_ref[:, GCOL_BG:GCOL_BG + V7X_LANES], preferred_element_type=F32)
        lane = lax.broadcasted_iota(jnp.int32, (tt, V7X_LANES), 1)
        gdec = -jnp.exp(alog_ref[...]) * _softplus(pb + dtb_ref[...])
        bg_s[...] = jnp.where(lane < nh, _sigmoid(pb), jnp.where(lane < 2 * nh, gdec, 0.0))
        yield
        for grp, dst in ((0, q_s), (1, k_s), (2, v_s)):
            cols = slice(grp * GDN_W, (grp + 1) * GDN_W)
            pbuf[HALO:HALO + tt, cols] = jnp.dot(h, win_ref[:, cols], preferred_element_type=F32)
            for hd in range(nh):
                c0 = grp * GDN_W + hd * GDN_D
                xe = pbuf[0:HALO + tt, c0:c0 + GDN_D]
                acc = xe * convw_ref[0:1, c0:c0 + GDN_D]
                for j in range(1, GDN_CONV):
                    acc = pltpu.roll(acc, 1, axis=0) + xe * convw_ref[j:j + 1, c0:c0 + GDN_D]
                y = _silu(acc[HALO:])
                if grp < 2:
                    y = y * lax.rsqrt(jnp.sum(y * y, axis=-1, keepdims=True) + L2_EPS)
                if grp == 0:
                    y = y * (GDN_D ** -0.5)
                dst[:, hd * GDN_D:(hd + 1) * GDN_D] = y
            yield
        pbuf[0:HALO, :] = pbuf[tt:tt + HALO, :]

        units = []
        for ci in range(nc):
            rows = slice(ci * c, (ci + 1) * c)
            bg = bg_s[rows, :]
            gc = _cumsum_rows(tri, bg)
            gct = gc.T
            for hd in range(nh):
                sl = slice(hd * GDN_D, (hd + 1) * GDN_D)
                gcol = gc[:, nh + hd:nh + hd + 1]
                grow = gct[nh + hd:nh + hd + 1, :]
                glast = gc[c - 1:c, nh + hd:nh + hd + 1]
                beta = bg[:, hd:hd + 1]
                k = k_s[rows, sl]
                units.append(dict(
                    ci=ci, hd=hd, gcol=gcol, glast=glast, k=k, q=q_s[rows, sl], kb=k * beta,
                    vb=v_s[rows, sl] * beta, eg=jnp.exp(gcol),
                    dec=jnp.where(incl, jnp.exp(jnp.where(incl, gcol - grow, 0.0)), 0.0)))
        yield
        kqs = [_dot_nt(jnp.concatenate([un["kb"], un["q"]], axis=0), un["k"]) for un in units]
        yield
        tinvs = yield from _unit_lower_inverses(
            [jnp.where(strict, kq[:c] * un["dec"], 0.0) for kq, un in zip(kqs, units)], c)
        uws = [_dot(tinv, jnp.concatenate([un["vb"], un["kb"] * un["eg"]], axis=1))
               for tinv, un in zip(tinvs, units)]
        yield
        for un, kq, uw in zip(units, kqs, uws):
            ci, hd = un["ci"], un["hd"]
            zl_scr[cur, ci, hd] = jnp.concatenate([uw[:, GDN_D:], un["q"] * un["eg"]], axis=0).astype(BF16)
            u_scr[cur, ci, hd] = uw[:, :GDN_D]
            aqk_scr[cur, ci, hd] = (kq[c:] * un["dec"]).astype(BF16)
            kdt_scr[cur, ci, hd] = (un["k"] * jnp.exp(un["glast"] - un["gcol"])).T.astype(BF16)
            egl_scr[cur, ci, hd] = jnp.broadcast_to(jnp.exp(un["glast"]), (8, GDN_D))

    def finish():
        for ci in range(nc):
            rows = slice(ci * c, (ci + 1) * c)
            zss = [_dot(zl_scr[prev, ci, hd], s_ref[hd]) for hd in range(nh)]
            yield
            v_news = [u_scr[prev, ci, hd] - zss[hd][:c] for hd in range(nh)]
            outs = [zss[hd][c:] + _dot(aqk_scr[prev, ci, hd], v_news[hd]) for hd in range(nh)]
            upds = [_dot(kdt_scr[prev, ci, hd], v_news[hd]) for hd in range(nh)]
            for hd in range(nh):
                s_ref[hd] = s_ref[hd] * egl_scr[prev, ci, hd][0:1, :] + upds[hd]
            for hd in range(nh):
                sl = slice(hd * GDN_D, (hd + 1) * GDN_D)
                o = outs[hd]
                o = o * lax.rsqrt(jnp.mean(o * o, axis=-1, keepdims=True) + NORM_EPS) * nw_ref[...]
                y_ref[rows, sl] = (o * _silu(z_scr[prev, rows, sl])).astype(y_ref.dtype)
            yield

    _interleave(prepare(), finish())


def _flat_tile_specs(n_t, n_tiles, tile, width_in, width_out):
    def in_map(s):
        i = jnp.minimum(s, n_tiles - 1)
        return (i // n_t, i % n_t, 0)

    def out_map(s):
        i = jnp.maximum(s - 1, 0)
        return (i // n_t, i % n_t, 0)

    return pl.BlockSpec((None, tile, width_in), in_map), pl.BlockSpec((None, tile, width_out), out_map)


def _gdn(x1, gmix, win, convw, alog, dtb, nw):
    b, t, _ = x1.shape
    tt = GDN_TILE
    nc = tt // GDN_CHUNK
    n_t = t // tt
    consts = (gmix, win, convw, alog, dtb, nw)
    in_spec, out_spec = _flat_tile_specs(n_t, b * n_t, tt, D_MODEL, GDN_W)
    return pl.pallas_call(
        functools.partial(_gdn_kernel, n_t=n_t),
        out_shape=jax.ShapeDtypeStruct((b, t, GDN_W), BF16),
        grid=(b * n_t + 1,),
        in_specs=[in_spec] + [_const_spec(c.shape) for c in consts],
        out_specs=out_spec,
        scratch_shapes=[pltpu.VMEM((tt + HALO, GCOL_Z), F32),
                        pltpu.VMEM((tt, GDN_W), F32), pltpu.VMEM((tt, GDN_W), F32), pltpu.VMEM((tt, GDN_W), F32),
                        pltpu.VMEM((tt, V7X_LANES), F32),
                        pltpu.VMEM((GDN_HEADS, GDN_D, GDN_D), F32),
                        pltpu.VMEM((2, tt, GDN_W), F32),
                        pltpu.VMEM((2, nc, GDN_HEADS, 2 * GDN_CHUNK, GDN_D), BF16),
                        pltpu.VMEM((2, nc, GDN_HEADS, GDN_CHUNK, GDN_D), F32),
                        pltpu.VMEM((2, nc, GDN_HEADS, GDN_CHUNK, GDN_CHUNK), BF16),
                        pltpu.VMEM((2, nc, GDN_HEADS, GDN_D, GDN_CHUNK), BF16),
                        pltpu.VMEM((2, nc, GDN_HEADS, 8, GDN_D), F32)],
        compiler_params=pltpu.CompilerParams(dimension_semantics=("arbitrary",), vmem_limit_bytes=VMEM_LIMIT),
        name="gdn",
    )(x1, *consts)


def _rwkv_kernel(x_ref, gmix_ref, win_ref, mu_ref, w0_ref, a0_ref, kkw_ref, kaw_ref, w2a2_ref, g2_ref,
                 rk_ref, gnw_ref, gnb_ref, y_ref,
                 pbuf, lw_s, kk_s, a_s, r_s, k_s, v_s, gate_s,
                 s_ref, zl_scr, ut_scr, yv_scr, arb_scr, kbt_scr, vb_scr, dm_scr, *, n_t):
    tt = x_ref.shape[0]
    c = RWKV_CHUNK
    n = RWKV_N
    npair = RWKV_HEADS // 2
    nc = tt // c
    first_step, seq_start, prev_seq_start, cur, prev = _tile_clock(n_t)

    @pl.when(first_step)
    def _():
        for ref in (r_s, k_s, v_s, gate_s, zl_scr, ut_scr, yv_scr, arb_scr, kbt_scr, vb_scr, dm_scr):
            ref[1] = jnp.zeros(ref.shape[1:], ref.dtype)

    @pl.when(first_step | prev_seq_start)
    def _():
        s_ref[...] = jnp.zeros(s_ref.shape, F32)

    @pl.when(seq_start)
    def _():
        pbuf[0:HALO, :] = jnp.zeros((HALO, pbuf.shape[1]), F32)

    lane_t = lax.broadcasted_iota(jnp.int32, (tt, V7X_LANES), 1)
    lo_half = lane_t < n

    def shifted(c0, width):
        now = pbuf[HALO:HALO + tt, c0:c0 + width]
        before = pbuf[HALO - 1:HALO - 1 + tt, c0:c0 + width]
        return now + (before - now) * mu_ref[:, c0:c0 + width]

    def project_group(h, c0, width):
        pbuf[HALO:HALO + tt, c0:c0 + width] = jnp.dot(h, win_ref[:, c0:c0 + width], preferred_element_type=F32)

    def project_and_pointwise():
        h = _rms(x_ref[...], gmix_ref[...]).astype(BF16)
        project_group(h, 3 * RWKV_W, LORA_WA + LORA_G)
        lwa = shifted(3 * RWKV_W, LORA_WA)
        lwa = jnp.where(lo_half, jnp.tanh(lwa), lwa)
        wa = jnp.dot(lwa.astype(BF16), w2a2_ref[...], preferred_element_type=F32)
        lg = shifted(3 * RWKV_W + LORA_WA, LORA_G)
        gate_s[cur] = jnp.dot(_sigmoid(lg).astype(BF16), g2_ref[...], preferred_element_type=F32)
        pairs = [slice(p * V7X_LANES, (p + 1) * V7X_LANES) for p in range(npair)]
        for p, sl in enumerate(pairs):
            w_log = -_softplus(-(w0_ref[:, sl] + wa[:, sl])) - 0.5
            lw_s[:, sl] = -jnp.exp(w_log)
            a_s[:, sl] = _sigmoid(a0_ref[:, sl] + wa[:, RWKV_W + p * V7X_LANES:RWKV_W + (p + 1) * V7X_LANES])
        yield
        project_group(h, 0, RWKV_W)
        for p, sl in enumerate(pairs):
            r_s[cur, :, sl] = shifted(p * V7X_LANES, V7X_LANES)
        yield
        project_group(h, RWKV_W, RWKV_W)
        for p, sl in enumerate(pairs):
            kr = shifted(RWKV_W + p * V7X_LANES, V7X_LANES)
            kx = kr * kkw_ref[:, sl]
            sq = kx * kx
            s_lo = jnp.sum(jnp.where(lo_half, sq, 0.0), axis=-1, keepdims=True)
            s_hi = jnp.sum(jnp.where(lo_half, 0.0, sq), axis=-1, keepdims=True)
            kk_s[:, sl] = kx * lax.rsqrt(jnp.where(lo_half, s_lo, s_hi) + L2_EPS)
            k_s[cur, :, sl] = kr * (1.0 + (a_s[:, sl] - 1.0) * kaw_ref[:, sl])
        yield
        project_group(h, 2 * RWKV_W, RWKV_W)
        for p, sl in enumerate(pairs):
            v_s[cur, :, sl] = shifted(2 * RWKV_W + p * V7X_LANES, V7X_LANES)
        yield
        pbuf[0:HALO, :] = pbuf[tt:tt + HALO, :]

    row_c = lax.broadcasted_iota(jnp.int32, (c, c), 0)
    col_c = lax.broadcasted_iota(jnp.int32, (c, c), 1)
    tri = jnp.where(row_c >= col_c, 1.0, 0.0).astype(BF16)
    row = lax.broadcasted_iota(jnp.int32, (c, 2 * n), 0)
    lane = lax.broadcasted_iota(jnp.int32, (c, 2 * n), 1)
    m0 = lane < n
    jj = jnp.where(m0, lane, lane - n)
    strict = row > jj
    incl = row >= jj
    row2 = lax.broadcasted_iota(jnp.int32, (2 * n, 2 * n), 0)
    lane2 = lax.broadcasted_iota(jnp.int32, (2 * n, 2 * n), 1)
    same_head = (row2 < n) == (lane2 < n)

    def sel0(x):
        return jnp.where(m0, x, 0.0)

    def sel1(x):
        return jnp.where(m0, 0.0, x)

    def head_sum(t):
        s_lo = jnp.sum(sel0(t), axis=-1, keepdims=True)
        s_hi = jnp.sum(sel1(t), axis=-1, keepdims=True)
        return jnp.where(m0, s_lo, s_hi)

    def prepare():
        yield from project_and_pointwise()
        units = []
        for ci in range(nc):
            rows = slice(ci * c, (ci + 1) * c)
            g_all = _cumsum_rows(tri, lw_s[rows, :])
            for p in range(npair):
                sl = slice(p * 2 * n, (p + 1) * 2 * n)
                g = g_all[:, sl]
                gmid = g[c // 2 - 1:c // 2, :]
                gend = g[c - 1:c, :]
                e_inv = jnp.exp(gmid - g)
                e_end = jnp.exp(gend - g)
                egm = jnp.exp(gmid)
                k = k_s[cur, rows, sl]
                kk = kk_s[rows, sl]
                b = kk * a_s[rows, sl]
                kap_t = kk * jnp.exp(g - lw_s[rows, sl] - gmid)
                r_t = r_s[cur, rows, sl] * jnp.exp(g - gmid)
                units.append(dict(ci=ci, p=p, v=v_s[cur, rows, sl], kap_t=kap_t, r_t=r_t, b_t=b * e_inv,
                                  k_t=k * e_inv, kb=jnp.concatenate([k * e_end, -(b * e_end)], axis=0), egm=egm,
                                  egend=jnp.exp(gend)))
        yield
        a0s = [_dot_nt(jnp.concatenate([sel0(un["kap_t"]), sel0(un["r_t"])], axis=0),
                       jnp.concatenate([un["b_t"], un["k_t"]], axis=0)) for un in units]
        yield
        a1s = [_dot_nt(jnp.concatenate([sel1(un["kap_t"]), sel1(un["r_t"])], axis=0),
                       jnp.concatenate([un["k_t"], un["b_t"]], axis=0)) for un in units]
        yield
        lmats = [jnp.concatenate([jnp.where(strict & m0, a0[:c], 0.0), jnp.where(strict & (~m0), a1[:c], 0.0)],
                                 axis=0) for a0, a1 in zip(a0s, a1s)]
        avys = []
        for un, a0, a1 in zip(units, a0s, a1s):
            a_uk = jnp.where(strict, jnp.where(m0, a1[:c], a0[:c]), 0.0)
            a_rk = jnp.where(incl, jnp.where(m0, a1[c:], a0[c:]), 0.0)
            v_sw = jnp.concatenate([sel1(un["v"]), sel0(un["v"])], axis=0)
            avys.append(_dot(jnp.concatenate([a_uk, a_rk], axis=0), v_sw))
        yield
        tbds = yield from _unit_lower_inverses(lmats, c)
        wus = []
        for un, tbd, avy in zip(units, tbds, avys):
            tcat = tbd[:c] + tbd[c:]
            kap_b = un["kap_t"] * un["egm"]
            av = avy[:c]
            x = jnp.concatenate([jnp.concatenate([sel0(kap_b), sel0(av)], axis=1),
                                 jnp.concatenate([sel1(kap_b), sel1(av)], axis=1)], axis=0)
            wus.append(_dot(tcat, x))
        yield
        for un, a0, a1, avy, wu in zip(units, a0s, a1s, avys, wus):
            ci, p = un["ci"], un["p"]
            zl_scr[cur, ci, p] = jnp.concatenate([wu[:, :2 * n], un["r_t"] * un["egm"]], axis=0).astype(BF16)
            ut_scr[cur, ci, p] = wu[:, 2 * n:]
            yv_scr[cur, ci, p] = avy[c:]
            arb_scr[cur, ci, p] = jnp.where(incl, jnp.where(m0, a0[c:], a1[c:]), 0.0).astype(BF16)
            kbt_scr[cur, ci, p] = un["kb"].T.astype(BF16)
            vb_scr[cur, ci, p] = un["v"].astype(BF16)
            dm_scr[cur, ci, p] = jnp.broadcast_to(un["egend"], (2 * n, 2 * n)).T

    def finish():
        for ci in range(nc):
            rows = slice(ci * c, (ci + 1) * c)
            zss = [_dot(zl_scr[prev, ci, p], s_ref[p]) for p in range(npair)]
            yield
            us = [zss[p][:c] + ut_scr[prev, ci, p] for p in range(npair)]
            ys = [zss[p][c:] - _dot(arb_scr[prev, ci, p], jnp.concatenate([sel0(us[p]), sel1(us[p])], axis=0))
                  + yv_scr[prev, ci, p] for p in range(npair)]
            upds = [jnp.dot(kbt_scr[prev, ci, p], jnp.concatenate([vb_scr[prev, ci, p], us[p].astype(BF16)], axis=0),
                            preferred_element_type=F32) for p in range(npair)]
            for p in range(npair):
                s_ref[p] = s_ref[p] * dm_scr[prev, ci, p] + jnp.where(same_head, upds[p], 0.0)
            for p in range(npair):
                sl = slice(p * 2 * n, (p + 1) * 2 * n)
                y = ys[p]
                mean = head_sum(y) * (1.0 / n)
                dlt = y - mean
                var = head_sum(dlt * dlt) * (1.0 / n)
                yn = dlt * lax.rsqrt(var + GN_EPS) * gnw_ref[:, sl] + gnb_ref[:, sl]
                bonus = head_sum(r_s[prev, rows, sl] * k_s[prev, rows, sl] * rk_ref[:, sl]) * v_s[prev, rows, sl]
                y_ref[rows, sl] = ((yn + bonus) * gate_s[prev, rows, sl]).astype(y_ref.dtype)
            yield

    _interleave(prepare(), finish())


def _rwkv(x1, gmix, win, mu, w0, a0, k_k, k_a, w2a2, g2, r_k, gn_w, gn_b):
    b, t, _ = x1.shape
    tt = RWKV_TILE
    nc = tt // RWKV_CHUNK
    n_t = t // tt
    npair = RWKV_HEADS // 2
    slab = 2 * RWKV_N
    consts = (gmix, win, mu, w0, a0, k_k, k_a, w2a2, g2, r_k, gn_w, gn_b)
    wide = pltpu.VMEM((tt, RWKV_W), F32)
    wide2 = pltpu.VMEM((2, tt, RWKV_W), F32)
    in_spec, out_spec = _flat_tile_specs(n_t, b * n_t, tt, D_MODEL, RWKV_W)
    return pl.pallas_call(
        functools.partial(_rwkv_kernel, n_t=n_t),
        out_shape=jax.ShapeDtypeStruct((b, t, RWKV_W), BF16),
        grid=(b * n_t + 1,),
        in_specs=[in_spec] + [_const_spec(c.shape) for c in consts],
        out_specs=out_spec,
        scratch_shapes=[pltpu.VMEM((tt + HALO, RW_COLS), F32), wide, wide, wide, wide2, wide2, wide2, wide2,
                        pltpu.VMEM((npair, slab, slab), F32),
                        pltpu.VMEM((2, nc, npair, 2 * RWKV_CHUNK, slab), BF16),
                        pltpu.VMEM((2, nc, npair, RWKV_CHUNK, slab), F32),
                        pltpu.VMEM((2, nc, npair, RWKV_CHUNK, slab), F32),
                        pltpu.VMEM((2, nc, npair, RWKV_CHUNK, slab), BF16),
                        pltpu.VMEM((2, nc, npair, slab, 2 * RWKV_CHUNK), BF16),
                        pltpu.VMEM((2, nc, npair, RWKV_CHUNK, slab), BF16),
                        pltpu.VMEM((2, nc, npair, slab, slab), F32)],
        compiler_params=pltpu.CompilerParams(dimension_semantics=("arbitrary",), vmem_limit_bytes=VMEM_LIMIT),
        name="rwkv",
    )(x1, *consts)


def _tail_stages(rows, x_ref, yg_ref, yr_ref, p_ref, wout_ref, gmp_ref, gpre_ref, wg_ref, wu_ref, wd_ref, gpost_ref,
                 gple_ref, wpg_ref, wpp_ref, gplep_ref, o_ref, a_scr):
    m = (jnp.dot(yg_ref[rows, :], wout_ref[0:GDN_W, :], preferred_element_type=F32)
         + jnp.dot(yr_ref[rows, :], wout_ref[GDN_W:GDN_W + RWKV_W, :], preferred_element_type=F32))
    yield
    x = x_ref[rows, :] + _rms(m, gmp_ref[...])
    h = _rms(x, gpre_ref[...]).astype(BF16)
    yield
    for s, w in FF_CHUNKS:
        g = jnp.dot(h, wg_ref[:, s:s + w], preferred_element_type=F32)
        u = jnp.dot(h, wu_ref[:, s:s + w], preferred_element_type=F32)
        a_scr[rows, s:s + w] = (_silu(g) * u).astype(BF16)
        yield
    f = jnp.dot(a_scr[rows, :], wd_ref[...], preferred_element_type=F32)
    yield
    x = x + 0.5 * _rms(f, gpost_ref[...])
    gate = _sigmoid(jnp.dot(_rms(x, gple_ref[...]).astype(BF16), wpg_ref[...], preferred_element_type=F32))
    emb = jnp.dot(p_ref[rows, :].astype(BF16), wpp_ref[...], preferred_element_type=F32)
    yield
    o_ref[rows, :] = x + _rms(gate * emb, gplep_ref[...])


def _tail_kernel(x_ref, *refs):
    part = x_ref.shape[0] // ROW_GROUPS
    _staggered([_tail_stages(slice(i * part, (i + 1) * part), x_ref, *refs) for i in range(ROW_GROUPS)])


def _tail(x1, yg, yr, p, wout, gmp, gpre, wg, wu, wd, gpost, gple, wpg, wpp, gplep, tm=1024):
    n = x1.shape[0]
    row = pl.BlockSpec((tm, D_MODEL), lambda i: (i, 0))
    half = pl.BlockSpec((tm, GDN_W), lambda i: (i, 0))
    ple = pl.BlockSpec((tm, D_PLE), lambda i: (i, 0))
    consts = (wout, gmp, gpre, wg, wu, wd, gpost, gple, wpg, wpp, gplep)
    return pl.pallas_call(
        _tail_kernel,
        out_shape=jax.ShapeDtypeStruct(x1.shape, F32),
        grid=(n // tm,),
        in_specs=[row, half, half, ple] + [_const_spec(c.shape) for c in consts],
        out_specs=row,
        scratch_shapes=[pltpu.VMEM((tm, D_FF), BF16)],
        compiler_params=pltpu.CompilerParams(dimension_semantics=("arbitrary",), vmem_limit_bytes=VMEM_LIMIT),
        name="tail",
    )(x1, yg, yr, p, *consts)


def _row(v):
    return v.reshape(1, -1).astype(F32)


def _layer(x, p, ffn1_norm_pre, ffn1_w_gate, ffn1_w_up, ffn1_w_down, ffn1_norm_post, mix_norm_pre, w_in,
           gdn_conv_w, gdn_a_log, gdn_dt_bias, gdn_norm_w, rwkv_mu, rwkv_w0, rwkv_w2, rwkv_a0, rwkv_a2, rwkv_g2,
           rwkv_k_k, rwkv_k_a, rwkv_r_k, rwkv_gn_w, rwkv_gn_b, w_out, mix_norm_post, ffn2_norm_pre, ffn2_w_gate,
           ffn2_w_up, ffn2_w_down, ffn2_norm_post, ple_norm_pre, ple_w_gate, ple_w_proj, ple_norm_post):
    b, t, d = x.shape
    n = b * t
    x1 = _ffn(x.reshape(n, d), _row(ffn1_norm_pre), ffn1_w_gate.astype(BF16), ffn1_w_up.astype(BF16),
              ffn1_w_down.astype(BF16), _row(ffn1_norm_post)).reshape(b, t, d)

    n_main = 4 * GDN_W
    n_bg = 2 * GDN_HEADS
    win_gdn = jnp.concatenate([w_in[:, :n_main + n_bg], jnp.zeros((d, V7X_LANES - n_bg), w_in.dtype)],
                              axis=1).astype(BF16)
    win_rw = w_in[:, n_main + n_bg:].astype(BF16)
    pad_bg = lambda vec: jnp.concatenate([jnp.zeros((GDN_HEADS,), F32), vec.astype(F32),
                                          jnp.zeros((V7X_LANES - n_bg,), F32)]).reshape(1, V7X_LANES)
    zero_l = jnp.zeros((RWKV_N, RWKV_W), F32)
    w2a2 = jnp.concatenate([jnp.concatenate([rwkv_w2.astype(F32), zero_l], axis=1),
                            jnp.concatenate([zero_l, rwkv_a2.astype(F32)], axis=1)], axis=0).astype(BF16)
    gmix = _row(mix_norm_pre)
    y_gdn = _gdn(x1, gmix, win_gdn, gdn_conv_w.astype(F32), pad_bg(gdn_a_log), pad_bg(gdn_dt_bias), _row(gdn_norm_w))
    y_rwkv = _rwkv(x1, gmix, win_rw, _row(rwkv_mu), _row(rwkv_w0), _row(rwkv_a0), _row(rwkv_k_k), _row(rwkv_k_a),
                   w2a2, rwkv_g2.astype(BF16), _row(rwkv_r_k), _row(rwkv_gn_w), _row(rwkv_gn_b))
    out = _tail(x1.reshape(n, d), y_gdn.reshape(n, GDN_W), y_rwkv.reshape(n, RWKV_W), p.reshape(n, D_PLE),
                w_out.astype(BF16), _row(mix_norm_post), _row(ffn2_norm_pre), ffn2_w_gate.astype(BF16),
                ffn2_w_up.astype(BF16), ffn2_w_down.astype(BF16), _row(ffn2_norm_post), _row(ple_norm_pre),
                ple_w_gate.astype(BF16), ple_w_proj.astype(BF16), _row(ple_norm_post))
    return out.reshape(b, t, d)


def kernel(x, p, ffn1_norm_pre, ffn1_w_gate, ffn1_w_up, ffn1_w_down, ffn1_norm_post, mix_norm_pre, w_in, gdn_conv_w, gdn_a_log, gdn_dt_bias, gdn_norm_w, rwkv_mu, rwkv_w0, rwkv_w2, rwkv_a0, rwkv_a2, rwkv_g2, rwkv_k_k, rwkv_k_a, rwkv_r_k, rwkv_gn_w, rwkv_gn_b, w_out, mix_norm_post, ffn2_norm_pre, ffn2_w_gate, ffn2_w_up, ffn2_w_down, ffn2_norm_post, ple_norm_pre, ple_w_gate, ple_w_proj, ple_norm_post):
    return _layer(x, p[0], ffn1_norm_pre[0], ffn1_w_gate[0], ffn1_w_up[0], ffn1_w_down[0], ffn1_norm_post[0],
                  mix_norm_pre[0], w_in[0], gdn_conv_w[0], gdn_a_log[0], gdn_dt_bias[0], gdn_norm_w[0], rwkv_mu[0],
                  rwkv_w0[0], rwkv_w2[0], rwkv_a0[0], rwkv_a2[0], rwkv_g2[0], rwkv_k_k[0], rwkv_k_a[0], rwkv_r_k[0],
                  rwkv_gn_w[0], rwkv_gn_b[0], w_out[0], mix_norm_post[0], ffn2_norm_pre[0], ffn2_w_gate[0],
                  ffn2_w_up[0], ffn2_w_down[0], ffn2_norm_post[0], ple_norm_pre[0], ple_w_gate[0], ple_w_proj[0],
                  ple_norm_post[0])
```

```python
import functools

import jax
import jax.numpy as jnp
from jax import lax
from jax.experimental import pallas as pl
from jax.experimental.pallas import tpu as pltpu

F32 = jnp.float32
BF16 = jnp.bfloat16

D_MODEL = 1024
D_FF = 2816
D_PLE = 256
GDN_HEADS = 4
GDN_D = 128
GDN_CONV = 4
RWKV_HEADS = 8
RWKV_N = 64
RWKV_W = RWKV_HEADS * RWKV_N
GDN_W = GDN_HEADS * GDN_D
LORA_WA = 128
LORA_G = 128
NORM_EPS = 1e-6
GN_EPS = 64e-5
L2_EPS = 1e-6

V7X_LANES = 128

GCOL_Z = 3 * GDN_W
GCOL_BG = 4 * GDN_W
RW_COLS = 3 * RWKV_W + LORA_WA + LORA_G

GDN_CHUNK = 128
RWKV_CHUNK = 64
HALO = 8
GDN_TILE = 512
RWKV_TILE = 256

FF_CHUNKS = ((0, 768), (768, 768), (1536, 768), (2304, 512))
ROW_GROUPS = 4
VMEM_LIMIT = 58 * 1024 * 1024


def _dot(a, b):
    return jnp.dot(a.astype(BF16), b.astype(BF16), preferred_element_type=F32)


def _dot_nt(a, b):
    return lax.dot_general(a.astype(BF16), b.astype(BF16), (((1,), (1,)), ((), ())),
                           preferred_element_type=F32)


def _rms(x, g):
    return x * lax.rsqrt(jnp.mean(x * x, axis=-1, keepdims=True) + NORM_EPS) * g


def _sigmoid(x):
    return 1.0 / (1.0 + jnp.exp(-x))


def _silu(x):
    h = 0.5 * x
    return h + h * jnp.tanh(h)


def _softplus(x):
    return jnp.maximum(x, 0.0) + jnp.log(1.0 + jnp.exp(-jnp.abs(x)))


def _split3(x):
    hi = x.astype(BF16)
    r1 = x - hi.astype(F32)
    mid = r1.astype(BF16)
    lo = (r1 - mid.astype(F32)).astype(BF16)
    return hi, mid, lo


def _cumsum_rows(tri, x):
    hi, mid, lo = _split3(x)
    return (jnp.dot(tri, hi, preferred_element_type=F32)
            + jnp.dot(tri, mid, preferred_element_type=F32)
            + jnp.dot(tri, lo, preferred_element_type=F32))


def _odd_blocks(x, b):
    return jnp.concatenate([x[s:s + b] for s in range(b, x.shape[0], 2 * b)], axis=0)


def _with_odd_blocks(base, odd, b):
    parts = []
    for j, s in enumerate(range(0, base.shape[0], 2 * b)):
        parts.append(base[s:s + b])
        parts.append(odd[j * b:(j + 1) * b])
    return jnp.concatenate(parts, axis=0)


def _unit_lower_inverses(lmats, top):
    n = lmats[0].shape[0]
    row = lax.broadcasted_iota(jnp.int32, (n, n), 0)
    col = lax.broadcasted_iota(jnp.int32, (n, n), 1)
    lower = row > col
    eye = jnp.where(row == col, 1.0, 0.0)
    xs = [eye - jnp.where(lower & ((row // 2) == (col // 2)), lm, 0.0) for lm in lmats]
    b = 2
    while b < top:
        sub = lower & ((row // (2 * b)) == (col // (2 * b))) & ((row // b) != (col // b))
        cs = [jnp.where(sub, lm, 0.0) for lm in lmats]
        if b % 8:
            ys = [_dot(c, x) for c, x in zip(cs, xs)]
            yield
            xs = [x - _dot(x, y) for x, y in zip(xs, ys)]
            yield
        else:
            zero = jnp.zeros((n, n), F32)
            ys = [_with_odd_blocks(zero, _dot(_odd_blocks(c, b), x), b) for c, x in zip(cs, xs)]
            yield
            xs = [_with_odd_blocks(x, _odd_blocks(x, b) - _dot(_odd_blocks(x, b), y), b) for x, y in zip(xs, ys)]
            yield
        b *= 2
    return xs


def _interleave(main, side):
    side_live = True
    while True:
        if side_live:
            try:
                next(side)
            except StopIteration:
                side_live = False
        try:
            next(main)
        except StopIteration:
            break
    if side_live:
        for _ in side:
            pass


def _const_spec(shape):
    return pl.BlockSpec(shape, lambda *_: (0,) * len(shape), pipeline_mode=pl.Buffered(1))


def _tile_clock(n_t):
    s = pl.program_id(0)
    cur = lax.rem(s, jnp.int32(2))
    t_cur = lax.rem(s, jnp.int32(n_t))
    t_prev = lax.rem(s + jnp.int32(n_t - 1), jnp.int32(n_t))
    return s == 0, t_cur == 0, t_prev == 0, cur, 1 - cur


def _ffn_stages(x_ref, rows, gpre_ref, wg_ref, wu_ref, wd_ref, gpost_ref, o_ref, a_scr):
    x = x_ref[rows, :]
    h = _rms(x, gpre_ref[...]).astype(BF16)
    yield
    for s, w in FF_CHUNKS:
        g = jnp.dot(h, wg_ref[:, s:s + w], preferred_element_type=F32)
        u = jnp.dot(h, wu_ref[:, s:s + w], preferred_element_type=F32)
        a_scr[rows, s:s + w] = (_silu(g) * u).astype(BF16)
        yield
    f = jnp.dot(a_scr[rows, :], wd_ref[...], preferred_element_type=F32)
    yield
    o_ref[rows, :] = x + 0.5 * _rms(f, gpost_ref[...])


def _staggered(chains):
    live = []
    waiting = list(chains)
    while waiting or live:
        if waiting:
            live.insert(0, waiting.pop(0))
        for gen in list(live):
            try:
                next(gen)
            except StopIteration:
                live.remove(gen)


def _ffn_kernel(x_ref, gpre_ref, wg_ref, wu_ref, wd_ref, gpost_ref, o_ref, a_scr):
    part = x_ref.shape[0] // ROW_GROUPS
    _staggered([_ffn_stages(x_ref, slice(i * part, (i + 1) * part), gpre_ref, wg_ref, wu_ref, wd_ref, gpost_ref,
                            o_ref, a_scr) for i in range(ROW_GROUPS)])


def _ffn(x2d, gpre, wg, wu, wd, gpost, tm=1024):
    n = x2d.shape[0]
    row = pl.BlockSpec((tm, D_MODEL), lambda i: (i, 0))
    return pl.pallas_call(
        _ffn_kernel,
        out_shape=jax.ShapeDtypeStruct(x2d.shape, F32),
        grid=(n // tm,),
        in_specs=[row, _const_spec((1, D_MODEL)), _const_spec((D_MODEL, D_FF)), _const_spec((D_MODEL, D_FF)),
                  _const_spec((D_FF, D_MODEL)), _const_spec((1, D_MODEL))],
        out_specs=row,
        scratch_shapes=[pltpu.VMEM((tm, D_FF), BF16)],
        compiler_params=pltpu.CompilerParams(dimension_semantics=("arbitrary",), vmem_limit_bytes=VMEM_LIMIT),
        name="ffn",
    )(x2d, gpre, wg, wu, wd, gpost)


def _gdn_kernel(x_ref, gmix_ref, win_ref, convw_ref, alog_ref, dtb_ref, nw_ref, y_ref,
                pbuf, q_s, k_s, v_s, bg_s, s_ref, z_scr, zl_scr, u_scr, aqk_scr, kdt_scr, egl_scr, *, n_t):
    tt = x_ref.shape[0]
    c = GDN_CHUNK
    nh = GDN_HEADS
    nc = tt // c
    first_step, seq_start, prev_seq_start, cur, prev = _tile_clock(n_t)

    @pl.when(first_step)
    def _():
        for ref in (z_scr, zl_scr, u_scr, aqk_scr, kdt_scr, egl_scr):
            ref[1] = jnp.zeros(ref.shape[1:], ref.dtype)

    @pl.when(first_step | prev_seq_start)
    def _():
        s_ref[...] = jnp.zeros(s_ref.shape, F32)

    @pl.when(seq_start)
    def _():
        pbuf[0:HALO, :] = jnp.zeros((HALO, pbuf.shape[1]), F32)

    row = lax.broadcasted_iota(jnp.int32, (c, c), 0)
    col = lax.broadcasted_iota(jnp.int32, (c, c), 1)
    incl = row >= col
    strict = row > col
    tri = jnp.where(incl, 1.0, 0.0).astype(BF16)

    def prepare():
        h = _rms(x_ref[...], gmix_ref[...]).astype(BF16)
        z_scr[cur] = jnp.dot(h, win_ref[:, GCOL_Z:GCOL_Z + GDN_W], preferred_element_type=F32)
        pb = jnp.dot(h, win_ref[:, GCOL_BG:GCOL_BG + V7X_LANES], preferred_element_type=F32)
        lane = lax.broadcasted_iota(jnp.int32, (tt, V7X_LANES), 1)
        gdec = -jnp.exp(alog_ref[...]) * _softplus(pb + dtb_ref[...])
        bg_s[...] = jnp.where(lane < nh, _sigmoid(pb), jnp.where(lane < 2 * nh, gdec, 0.0))
        yield
        for grp, dst in ((0, q_s), (1, k_s), (2, v_s)):
            cols = slice(grp * GDN_W, (grp + 1) * GDN_W)
            pbuf[HALO:HALO + tt, cols] = jnp.dot(h, win_ref[:, cols], preferred_element_type=F32)
            for hd in range(nh):
                c0 = grp * GDN_W + hd * GDN_D
                xe = pbuf[0:HALO + tt, c0:c0 + GDN_D]
                acc = xe * convw_ref[0:1, c0:c0 + GDN_D]
                for j in range(1, GDN_CONV):
                    acc = pltpu.roll(acc, 1, axis=0) + xe * convw_ref[j:j + 1, c0:c0 + GDN_D]
                y = _silu(acc[HALO:])
                if grp < 2:
                    y = y * lax.rsqrt(jnp.sum(y * y, axis=-1, keepdims=True) + L2_EPS)
                if grp == 0:
                    y = y * (GDN_D ** -0.5)
                dst[:, hd * GDN_D:(hd + 1) * GDN_D] = y
            yield
        pbuf[0:HALO, :] = pbuf[tt:tt + HALO, :]

        units = []
        for ci in range(nc):
            rows = slice(ci * c, (ci + 1) * c)
            bg = bg_s[rows, :]
            gc = _cumsum_rows(tri, bg)
            gct = gc.T
            for hd in range(nh):
                sl = slice(hd * GDN_D, (hd + 1) * GDN_D)
                gcol = gc[:, nh + hd:nh + hd + 1]
                grow = gct[nh + hd:nh + hd + 1, :]
                glast = gc[c - 1:c, nh + hd:nh + hd + 1]
                beta = bg[:, hd:hd + 1]
                k = k_s[rows, sl]
                units.append(dict(
                    ci=ci, hd=hd, gcol=gcol, glast=glast, k=k, q=q_s[rows, sl], kb=k * beta,
                    vb=v_s[rows, sl] * beta, eg=jnp.exp(gcol),
                    dec=jnp.where(incl, jnp.exp(jnp.where(incl, gcol - grow, 0.0)), 0.0)))
        yield
        kqs = [_dot_nt(jnp.concatenate([un["kb"], un["q"]], axis=0), un["k"]) for un in units]
        yield
        tinvs = yield from _unit_lower_inverses(
            [jnp.where(strict, kq[:c] * un["dec"], 0.0) for kq, un in zip(kqs, units)], c)
        uws = [_dot(tinv, jnp.concatenate([un["vb"], un["kb"] * un["eg"]], axis=1))
               for tinv, un in zip(tinvs, units)]
        yield
        for un, kq, uw in zip(units, kqs, uws):
            ci, hd = un["ci"], un["hd"]
            zl_scr[cur, ci, hd] = jnp.concatenate([uw[:, GDN_D:], un["q"] * un["eg"]], axis=0).astype(BF16)
            u_scr[cur, ci, hd] = uw[:, :GDN_D]
            aqk_scr[cur, ci, hd] = (kq[c:] * un["dec"]).astype(BF16)
            kdt_scr[cur, ci, hd] = (un["k"] * jnp.exp(un["glast"] - un["gcol"])).T.astype(BF16)
            egl_scr[cur, ci, hd] = jnp.broadcast_to(jnp.exp(un["glast"]), (8, GDN_D))

    def finish():
        for ci in range(nc):
            rows = slice(ci * c, (ci + 1) * c)
            zss = [_dot(zl_scr[prev, ci, hd], s_ref[hd]) for hd in range(nh)]
            yield
            v_news = [u_scr[prev, ci, hd] - zss[hd][:c] for hd in range(nh)]
            outs = [zss[hd][c:] + _dot(aqk_scr[prev, ci, hd], v_news[hd]) for hd in range(nh)]
            upds = [_dot(kdt_scr[prev, ci, hd], v_news[hd]) for hd in range(nh)]
            for hd in range(nh):
                s_ref[hd] = s_ref[hd] * egl_scr[prev, ci, hd][0:1, :] + upds[hd]
            for hd in range(nh):
                sl = slice(hd * GDN_D, (hd + 1) * GDN_D)
                o = outs[hd]
                o = o * lax.rsqrt(jnp.mean(o * o, axis=-1, keepdims=True) + NORM_EPS) * nw_ref[...]
                y_ref[rows, sl] = (o * _silu(z_scr[prev, rows, sl])).astype(y_ref.dtype)
            yield

    _interleave(prepare(), finish())


def _flat_tile_specs(n_t, n_tiles, tile, width_in, width_out):
    def in_map(s):
        i = jnp.minimum(s, n_tiles - 1)
        return (i // n_t, i % n_t, 0)

    def out_map(s):
        i = jnp.maximum(s - 1, 0)
        return (i // n_t, i % n_t, 0)

    return pl.BlockSpec((None, tile, width_in), in_map), pl.BlockSpec((None, tile, width_out), out_map)


def _gdn(x1, gmix, win, convw, alog, dtb, nw):
    b, t, _ = x1.shape
    tt = GDN_TILE
    nc = tt // GDN_CHUNK
    n_t = t // tt
    consts = (gmix, win, convw, alog, dtb, nw)
    in_spec, out_spec = _flat_tile_specs(n_t, b * n_t, tt, D_MODEL, GDN_W)
    return pl.pallas_call(
        functools.partial(_gdn_kernel, n_t=n_t),
        out_shape=jax.ShapeDtypeStruct((b, t, GDN_W), BF16),
        grid=(b * n_t + 1,),
        in_specs=[in_spec] + [_const_spec(c.shape) for c in consts],
        out_specs=out_spec,
        scratch_shapes=[pltpu.VMEM((tt + HALO, GCOL_Z), F32),
                        pltpu.VMEM((tt, GDN_W), F32), pltpu.VMEM((tt, GDN_W), F32), pltpu.VMEM((tt, GDN_W), F32),
                        pltpu.VMEM((tt, V7X_LANES), F32),
                        pltpu.VMEM((GDN_HEADS, GDN_D, GDN_D), F32),
                        pltpu.VMEM((2, tt, GDN_W), F32),
                        pltpu.VMEM((2, nc, GDN_HEADS, 2 * GDN_CHUNK, GDN_D), BF16),
                        pltpu.VMEM((2, nc, GDN_HEADS, GDN_CHUNK, GDN_D), F32),
                        pltpu.VMEM((2, nc, GDN_HEADS, GDN_CHUNK, GDN_CHUNK), BF16),
                        pltpu.VMEM((2, nc, GDN_HEADS, GDN_D, GDN_CHUNK), BF16),
                        pltpu.VMEM((2, nc, GDN_HEADS, 8, GDN_D), F32)],
        compiler_params=pltpu.CompilerParams(dimension_semantics=("arbitrary",), vmem_limit_bytes=VMEM_LIMIT),
        name="gdn",
    )(x1, *consts)


def _rwkv_kernel(x_ref, gmix_ref, win_ref, mu_ref, w0_ref, a0_ref, kkw_ref, kaw_ref, w2a2_ref, g2_ref,
                 rk_ref, gnw_ref, gnb_ref, y_ref,
                 pbuf, lw_s, kk_s, a_s, r_s, k_s, v_s, gate_s,
                 s_ref, zl_scr, ut_scr, yv_scr, arb_scr, kbt_scr, vb_scr, dm_scr, *, n_t):
    tt = x_ref.shape[0]
    c = RWKV_CHUNK
    n = RWKV_N
    npair = RWKV_HEADS // 2
    nc = tt // c
    first_step, seq_start, prev_seq_start, cur, prev = _tile_clock(n_t)

    @pl.when(first_step)
    def _():
        for ref in (r_s, k_s, v_s, gate_s, zl_scr, ut_scr, yv_scr, arb_scr, kbt_scr, vb_scr, dm_scr):
            ref[1] = jnp.zeros(ref.shape[1:], ref.dtype)

    @pl.when(first_step | prev_seq_start)
    def _():
        s_ref[...] = jnp.zeros(s_ref.shape, F32)

    @pl.when(seq_start)
    def _():
        pbuf[0:HALO, :] = jnp.zeros((HALO, pbuf.shape[1]), F32)

    lane_t = lax.broadcasted_iota(jnp.int32, (tt, V7X_LANES), 1)
    lo_half = lane_t < n

    def shifted(c0, width):
        now = pbuf[HALO:HALO + tt, c0:c0 + width]
        before = pbuf[HALO - 1:HALO - 1 + tt, c0:c0 + width]
        return now + (before - now) * mu_ref[:, c0:c0 + width]

    def project_group(h, c0, width):
        pbuf[HALO:HALO + tt, c0:c0 + width] = jnp.dot(h, win_ref[:, c0:c0 + width], preferred_element_type=F32)

    def project_and_pointwise():
        h = _rms(x_ref[...], gmix_ref[...]).astype(BF16)
        project_group(h, 3 * RWKV_W, LORA_WA + LORA_G)
        lwa = shifted(3 * RWKV_W, LORA_WA)
        lwa = jnp.where(lo_half, jnp.tanh(lwa), lwa)
        wa = jnp.dot(lwa.astype(BF16), w2a2_ref[...], preferred_element_type=F32)
        lg = shifted(3 * RWKV_W + LORA_WA, LORA_G)
        gate_s[cur] = jnp.dot(_sigmoid(lg).astype(BF16), g2_ref[...], preferred_element_type=F32)
        pairs = [slice(p * V7X_LANES, (p + 1) * V7X_LANES) for p in range(npair)]
        for p, sl in enumerate(pairs):
            w_log = -_softplus(-(w0_ref[:, sl] + wa[:, sl])) - 0.5
            lw_s[:, sl] = -jnp.exp(w_log)
            a_s[:, sl] = _sigmoid(a0_ref[:, sl] + wa[:, RWKV_W + p * V7X_LANES:RWKV_W + (p + 1) * V7X_LANES])
        yield
        project_group(h, 0, RWKV_W)
        for p, sl in enumerate(pairs):
            r_s[cur, :, sl] = shifted(p * V7X_LANES, V7X_LANES)
        yield
        project_group(h, RWKV_W, RWKV_W)
        for p, sl in enumerate(pairs):
            kr = shifted(RWKV_W + p * V7X_LANES, V7X_LANES)
            kx = kr * kkw_ref[:, sl]
            sq = kx * kx
            s_lo = jnp.sum(jnp.where(lo_half, sq, 0.0), axis=-1, keepdims=True)
            s_hi = jnp.sum(jnp.where(lo_half, 0.0, sq), axis=-1, keepdims=True)
            kk_s[:, sl] = kx * lax.rsqrt(jnp.where(lo_half, s_lo, s_hi) + L2_EPS)
            k_s[cur, :, sl] = kr * (1.0 + (a_s[:, sl] - 1.0) * kaw_ref[:, sl])
        yield
        project_group(h, 2 * RWKV_W, RWKV_W)
        for p, sl in enumerate(pairs):
            v_s[cur, :, sl] = shifted(2 * RWKV_W + p * V7X_LANES, V7X_LANES)
        yield
        pbuf[0:HALO, :] = pbuf[tt:tt + HALO, :]

    row_c = lax.broadcasted_iota(jnp.int32, (c, c), 0)
    col_c = lax.broadcasted_iota(jnp.int32, (c, c), 1)
    tri = jnp.where(row_c >= col_c, 1.0, 0.0).astype(BF16)
    row = lax.broadcasted_iota(jnp.int32, (c, 2 * n), 0)
    lane = lax.broadcasted_iota(jnp.int32, (c, 2 * n), 1)
    m0 = lane < n
    jj = jnp.where(m0, lane, lane - n)
    strict = row > jj
    incl = row >= jj
    row2 = lax.broadcasted_iota(jnp.int32, (2 * n, 2 * n), 0)
    lane2 = lax.broadcasted_iota(jnp.int32, (2 * n, 2 * n), 1)
    same_head = (row2 < n) == (lane2 < n)

    def sel0(x):
        return jnp.where(m0, x, 0.0)

    def sel1(x):
        return jnp.where(m0, 0.0, x)

    def head_sum(t):
        s_lo = jnp.sum(sel0(t), axis=-1, keepdims=True)
        s_hi = jnp.sum(sel1(t), axis=-1, keepdims=True)
        return jnp.where(m0, s_lo, s_hi)

    def prepare():
        yield from project_and_pointwise()
        units = []
        for ci in range(nc):
            rows = slice(ci * c, (ci + 1) * c)
            g_all = _cumsum_rows(tri, lw_s[rows, :])
            for p in range(npair):
                sl = slice(p * 2 * n, (p + 1) * 2 * n)
                g = g_all[:, sl]
                gmid = g[c // 2 - 1:c // 2, :]
                gend = g[c - 1:c, :]
                e_inv = jnp.exp(gmid - g)
                e_end = jnp.exp(gend - g)
                egm = jnp.exp(gmid)
                k = k_s[cur, rows, sl]
                kk = kk_s[rows, sl]
                b = kk * a_s[rows, sl]
                kap_t = kk * jnp.exp(g - lw_s[rows, sl] - gmid)
                r_t = r_s[cur, rows, sl] * jnp.exp(g - gmid)
                units.append(dict(ci=ci, p=p, v=v_s[cur, rows, sl], kap_t=kap_t, r_t=r_t, b_t=b * e_inv,
                                  k_t=k * e_inv, kb=jnp.concatenate([k * e_end, -(b * e_end)], axis=0), egm=egm,
                                  egend=jnp.exp(gend)))
        yield
        a0s = [_dot_nt(jnp.concatenate([sel0(un["kap_t"]), sel0(un["r_t"])], axis=0),
                       jnp.concatenate([un["b_t"], un["k_t"]], axis=0)) for un in units]
        yield
        a1s = [_dot_nt(jnp.concatenate([sel1(un["kap_t"]), sel1(un["r_t"])], axis=0),
                       jnp.concatenate([un["k_t"], un["b_t"]], axis=0)) for un in units]
        yield
        lmats = [jnp.concatenate([jnp.where(strict & m0, a0[:c], 0.0), jnp.where(strict & (~m0), a1[:c], 0.0)],
                                 axis=0) for a0, a1 in zip(a0s, a1s)]
        avys = []
        for un, a0, a1 in zip(units, a0s, a1s):
            a_uk = jnp.where(strict, jnp.where(m0, a1[:c], a0[:c]), 0.0)
            a_rk = jnp.where(incl, jnp.where(m0, a1[c:], a0[c:]), 0.0)
            v_sw = jnp.concatenate([sel1(un["v"]), sel0(un["v"])], axis=0)
            avys.append(_dot(jnp.concatenate([a_uk, a_rk], axis=0), v_sw))
        yield
        tbds = yield from _unit_lower_inverses(lmats, c)
        wus = []
        for un, tbd, avy in zip(units, tbds, avys):
            tcat = tbd[:c] + tbd[c:]
            kap_b = un["kap_t"] * un["egm"]
            av = avy[:c]
            x = jnp.concatenate([jnp.concatenate([sel0(kap_b), sel0(av)], axis=1),
                                 jnp.concatenate([sel1(kap_b), sel1(av)], axis=1)], axis=0)
            wus.append(_dot(tcat, x))
        yield
        for un, a0, a1, avy, wu in zip(units, a0s, a1s, avys, wus):
            ci, p = un["ci"], un["p"]
            zl_scr[cur, ci, p] = jnp.concatenate([wu[:, :2 * n], un["r_t"] * un["egm"]], axis=0).astype(BF16)
            ut_scr[cur, ci, p] = wu[:, 2 * n:]
            yv_scr[cur, ci, p] = avy[c:]
            arb_scr[cur, ci, p] = jnp.where(incl, jnp.where(m0, a0[c:], a1[c:]), 0.0).astype(BF16)
            kbt_scr[cur, ci, p] = un["kb"].T.astype(BF16)
            vb_scr[cur, ci, p] = un["v"].astype(BF16)
            dm_scr[cur, ci, p] = jnp.broadcast_to(un["egend"], (2 * n, 2 * n)).T

    def finish():
        for ci in range(nc):
            rows = slice(ci * c, (ci + 1) * c)
            zss = [_dot(zl_scr[prev, ci, p], s_ref[p]) for p in range(npair)]
            yield
            us = [zss[p][:c] + ut_scr[prev, ci, p] for p in range(npair)]
            ys = [zss[p][c:] - _dot(arb_scr[prev, ci, p], jnp.concatenate([sel0(us[p]), sel1(us[p])], axis=0))
                  + yv_scr[prev, ci, p] for p in range(npair)]
            upds = [jnp.dot(kbt_scr[prev, ci, p], jnp.concatenate([vb_scr[prev, ci, p], us[p].astype(BF16)], axis=0),
                            preferred_element_type=F32) for p in range(npair)]
            for p in range(npair):
                s_ref[p] = s_ref[p] * dm_scr[prev, ci, p] + jnp.where(same_head, upds[p], 0.0)
            for p in range(npair):
                sl = slice(p * 2 * n, (p + 1) * 2 * n)
                y = ys[p]
                mean = head_sum(y) * (1.0 / n)
                dlt = y - mean
                var = head_sum(dlt * dlt) * (1.0 / n)
                yn = dlt * lax.rsqrt(var + GN_EPS) * gnw_ref[:, sl] + gnb_ref[:, sl]
                bonus = head_sum(r_s[prev, rows, sl] * k_s[prev, rows, sl] * rk_ref[:, sl]) * v_s[prev, rows, sl]
                y_ref[rows, sl] = ((yn + bonus) * gate_s[prev, rows, sl]).astype(y_ref.dtype)
            yield

    _interleave(prepare(), finish())


def _rwkv(x1, gmix, win, mu, w0, a0, k_k, k_a, w2a2, g2, r_k, gn_w, gn_b):
    b, t, _ = x1.shape
    tt = RWKV_TILE
    nc = tt // RWKV_CHUNK
    n_t = t // tt
    npair = RWKV_HEADS // 2
    slab = 2 * RWKV_N
    consts = (gmix, win, mu, w0, a0, k_k, k_a, w2a2, g2, r_k, gn_w, gn_b)
    wide = pltpu.VMEM((tt, RWKV_W), F32)
    wide2 = pltpu.VMEM((2, tt, RWKV_W), F32)
    in_spec, out_spec = _flat_tile_specs(n_t, b * n_t, tt, D_MODEL, RWKV_W)
    return pl.pallas_call(
        functools.partial(_rwkv_kernel, n_t=n_t),
        out_shape=jax.ShapeDtypeStruct((b, t, RWKV_W), BF16),
        grid=(b * n_t + 1,),
        in_specs=[in_spec] + [_const_spec(c.shape) for c in consts],
        out_specs=out_spec,
        scratch_shapes=[pltpu.VMEM((tt + HALO, RW_COLS), F32), wide, wide, wide, wide2, wide2, wide2, wide2,
                        pltpu.VMEM((npair, slab, slab), F32),
                        pltpu.VMEM((2, nc, npair, 2 * RWKV_CHUNK, slab), BF16),
                        pltpu.VMEM((2, nc, npair, RWKV_CHUNK, slab), F32),
                        pltpu.VMEM((2, nc, npair, RWKV_CHUNK, slab), F32),
                        pltpu.VMEM((2, nc, npair, RWKV_CHUNK, slab), BF16),
                        pltpu.VMEM((2, nc, npair, slab, 2 * RWKV_CHUNK), BF16),
                        pltpu.VMEM((2, nc, npair, RWKV_CHUNK, slab), BF16),
                        pltpu.VMEM((2, nc, npair, slab, slab), F32)],
        compiler_params=pltpu.CompilerParams(dimension_semantics=("arbitrary",), vmem_limit_bytes=VMEM_LIMIT),
        name="rwkv",
    )(x1, *consts)


def _tail_stages(rows, x_ref, yg_ref, yr_ref, p_ref, wout_ref, gmp_ref, gpre_ref, wg_ref, wu_ref, wd_ref, gpost_ref,
                 gple_ref, wpg_ref, wpp_ref, gplep_ref, o_ref, a_scr):
    m = (jnp.dot(yg_ref[rows, :], wout_ref[0:GDN_W, :], preferred_element_type=F32)
         + jnp.dot(yr_ref[rows, :], wout_ref[GDN_W:GDN_W + RWKV_W, :], preferred_element_type=F32))
    yield
    x = x_ref[rows, :] + _rms(m, gmp_ref[...])
    h = _rms(x, gpre_ref[...]).astype(BF16)
    yield
    for s, w in FF_CHUNKS:
        g = jnp.dot(h, wg_ref[:, s:s + w], preferred_element_type=F32)
        u = jnp.dot(h, wu_ref[:, s:s + w], preferred_element_type=F32)
        a_scr[rows, s:s + w] = (_silu(g) * u).astype(BF16)
        yield
    f = jnp.dot(a_scr[rows, :], wd_ref[...], preferred_element_type=F32)
    yield
    x = x + 0.5 * _rms(f, gpost_ref[...])
    gate = _sigmoid(jnp.dot(_rms(x, gple_ref[...]).astype(BF16), wpg_ref[...], preferred_element_type=F32))
    emb = jnp.dot(p_ref[rows, :].astype(BF16), wpp_ref[...], preferred_element_type=F32)
    yield
    o_ref[rows, :] = x + _rms(gate * emb, gplep_ref[...])


def _tail_kernel(x_ref, *refs):
    part = x_ref.shape[0] // ROW_GROUPS
    _staggered([_tail_stages(slice(i * part, (i + 1) * part), x_ref, *refs) for i in range(ROW_GROUPS)])


def _tail(x1, yg, yr, p, wout, gmp, gpre, wg, wu, wd, gpost, gple, wpg, wpp, gplep, tm=1024):
    n = x1.shape[0]
    row = pl.BlockSpec((tm, D_MODEL), lambda i: (i, 0))
    half = pl.BlockSpec((tm, GDN_W), lambda i: (i, 0))
    ple = pl.BlockSpec((tm, D_PLE), lambda i: (i, 0))
    consts = (wout, gmp, gpre, wg, wu, wd, gpost, gple, wpg, wpp, gplep)
    return pl.pallas_call(
        _tail_kernel,
        out_shape=jax.ShapeDtypeStruct(x1.shape, F32),
        grid=(n // tm,),
        in_specs=[row, half, half, ple] + [_const_spec(c.shape) for c in consts],
        out_specs=row,
        scratch_shapes=[pltpu.VMEM((tm, D_FF), BF16)],
        compiler_params=pltpu.CompilerParams(dimension_semantics=("arbitrary",), vmem_limit_bytes=VMEM_LIMIT),
        name="tail",
    )(x1, yg, yr, p, *consts)


def _row(v):
    return v.reshape(1, -1).astype(F32)


def _layer(x, p, ffn1_norm_pre, ffn1_w_gate, ffn1_w_up, ffn1_w_down, ffn1_norm_post, mix_norm_pre, w_in,
           gdn_conv_w, gdn_a_log, gdn_dt_bias, gdn_norm_w, rwkv_mu, rwkv_w0, rwkv_w2, rwkv_a0, rwkv_a2, rwkv_g2,
           rwkv_k_k, rwkv_k_a, rwkv_r_k, rwkv_gn_w, rwkv_gn_b, w_out, mix_norm_post, ffn2_norm_pre, ffn2_w_gate,
           ffn2_w_up, ffn2_w_down, ffn2_norm_post, ple_norm_pre, ple_w_gate, ple_w_proj, ple_norm_post):
    b, t, d = x.shape
    n = b * t
    x1 = _ffn(x.reshape(n, d), _row(ffn1_norm_pre), ffn1_w_gate.astype(BF16), ffn1_w_up.astype(BF16),
              ffn1_w_down.astype(BF16), _row(ffn1_norm_post)).reshape(b, t, d)

    n_main = 4 * GDN_W
    n_bg = 2 * GDN_HEADS
    win_gdn = jnp.concatenate([w_in[:, :n_main + n_bg], jnp.zeros((d, V7X_LANES - n_bg), w_in.dtype)],
                              axis=1).astype(BF16)
    win_rw = w_in[:, n_main + n_bg:].astype(BF16)
    pad_bg = lambda vec: jnp.concatenate([jnp.zeros((GDN_HEADS,), F32), vec.astype(F32),
                                          jnp.zeros((V7X_LANES - n_bg,), F32)]).reshape(1, V7X_LANES)
    zero_l = jnp.zeros((RWKV_N, RWKV_W), F32)
    w2a2 = jnp.concatenate([jnp.concatenate([rwkv_w2.astype(F32), zero_l], axis=1),
                            jnp.concatenate([zero_l, rwkv_a2.astype(F32)], axis=1)], axis=0).astype(BF16)
    gmix = _row(mix_norm_pre)
    y_gdn = _gdn(x1, gmix, win_gdn, gdn_conv_w.astype(F32), pad_bg(gdn_a_log), pad_bg(gdn_dt_bias), _row(gdn_norm_w))
    y_rwkv = _rwkv(x1, gmix, win_rw, _row(rwkv_mu), _row(rwkv_w0), _row(rwkv_a0), _row(rwkv_k_k), _row(rwkv_k_a),
                   w2a2, rwkv_g2.astype(BF16), _row(rwkv_r_k), _row(rwkv_gn_w), _row(rwkv_gn_b))
    out = _tail(x1.reshape(n, d), y_gdn.reshape(n, GDN_W), y_rwkv.reshape(n, RWKV_W), p.reshape(n, D_PLE),
                w_out.astype(BF16), _row(mix_norm_post), _row(ffn2_norm_pre), ffn2_w_gate.astype(BF16),
                ffn2_w_up.astype(BF16), ffn2_w_down.astype(BF16), _row(ffn2_norm_post), _row(ple_norm_pre),
                ple_w_gate.astype(BF16), ple_w_proj.astype(BF16), _row(ple_norm_post))
    return out.reshape(b, t, d)


def kernel(x, p, ffn1_norm_pre, ffn1_w_gate, ffn1_w_up, ffn1_w_down, ffn1_norm_post, mix_norm_pre, w_in, gdn_conv_w, gdn_a_log, gdn_dt_bias, gdn_norm_w, rwkv_mu, rwkv_w0, rwkv_w2, rwkv_a0, rwkv_a2, rwkv_g2, rwkv_k_k, rwkv_k_a, rwkv_r_k, rwkv_gn_w, rwkv_gn_b, w_out, mix_norm_post, ffn2_norm_pre, ffn2_w_gate, ffn2_w_up, ffn2_w_down, ffn2_norm_post, ple_norm_pre, ple_w_gate, ple_w_proj, ple_norm_post):
    stacked = (p, ffn1_norm_pre, ffn1_w_gate, ffn1_w_up, ffn1_w_down, ffn1_norm_post, mix_norm_pre, w_in, gdn_conv_w,
               gdn_a_log, gdn_dt_bias, gdn_norm_w, rwkv_mu, rwkv_w0, rwkv_w2, rwkv_a0, rwkv_a2, rwkv_g2, rwkv_k_k,
               rwkv_k_a, rwkv_r_k, rwkv_gn_w, rwkv_gn_b, w_out, mix_norm_post, ffn2_norm_pre, ffn2_w_gate, ffn2_w_up,
               ffn2_w_down, ffn2_norm_post, ple_norm_pre, ple_w_gate, ple_w_proj, ple_norm_post)
    for i in range(p.shape[0]):
        x = _layer(x, *(t[i] for t in stacked))
    return x
```

```python
import functools

import jax
import jax.numpy as jnp
from jax import lax
from jax.experimental import pallas as pl
from jax.experimental.pallas import tpu as pltpu

F32 = jnp.float32
BF16 = jnp.bfloat16

D_MODEL = 1024
D_FF = 2816
D_PLE = 256
GDN_HEADS = 4
GDN_D = 128
GDN_CONV = 4
RWKV_HEADS = 8
RWKV_N = 64
RWKV_W = RWKV_HEADS * RWKV_N
GDN_W = GDN_HEADS * GDN_D
LORA_WA = 128
LORA_G = 128
NORM_EPS = 1e-6
GN_EPS = 64e-5
L2_EPS = 1e-6

V7X_LANES = 128

GCOL_Z = 3 * GDN_W
GCOL_BG = 4 * GDN_W
GDN_COLS = GCOL_BG + V7X_LANES
RW_COLS = 3 * RWKV_W + LORA_WA + LORA_G

GDN_CHUNK = 128
RWKV_CHUNK = 64
HALO = 8
GDN_TILE = 512
RWKV_TILE = 256

FF_CHUNKS = ((0, 768), (768, 768), (1536, 768), (2304, 512))
ROW_GROUPS = 4
VMEM_LIMIT = 58 * 1024 * 1024


def _dot(a, b):
    return jnp.dot(a.astype(BF16), b.astype(BF16), preferred_element_type=F32)


def _dot_nt(a, b):
    return lax.dot_general(a.astype(BF16), b.astype(BF16), (((1,), (1,)), ((), ())),
                           preferred_element_type=F32)


def _rms(x, g):
    return x * lax.rsqrt(jnp.mean(x * x, axis=-1, keepdims=True) + NORM_EPS) * g


def _sigmoid(x):
    return 1.0 / (1.0 + jnp.exp(-x))


def _silu(x):
    h = 0.5 * x
    return h + h * jnp.tanh(h)


def _softplus(x):
    return jnp.maximum(x, 0.0) + jnp.log(1.0 + jnp.exp(-jnp.abs(x)))


def _split3(x):
    hi = x.astype(BF16)
    r1 = x - hi.astype(F32)
    mid = r1.astype(BF16)
    lo = (r1 - mid.astype(F32)).astype(BF16)
    return hi, mid, lo


def _cumsum_rows(tri, x):
    hi, mid, lo = _split3(x)
    return (jnp.dot(tri, hi, preferred_element_type=F32)
            + jnp.dot(tri, mid, preferred_element_type=F32)
            + jnp.dot(tri, lo, preferred_element_type=F32))


def _odd_blocks(x, b):
    return jnp.concatenate([x[s:s + b] for s in range(b, x.shape[0], 2 * b)], axis=0)


def _with_odd_blocks(base, odd, b):
    parts = []
    for j, s in enumerate(range(0, base.shape[0], 2 * b)):
        parts.append(base[s:s + b])
        parts.append(odd[j * b:(j + 1) * b])
    return jnp.concatenate(parts, axis=0)


def _unit_lower_inverses(lmats, top):
    n = lmats[0].shape[0]
    row = lax.broadcasted_iota(jnp.int32, (n, n), 0)
    col = lax.broadcasted_iota(jnp.int32, (n, n), 1)
    lower = row > col
    eye = jnp.where(row == col, 1.0, 0.0)
    xs = [eye - jnp.where(lower & ((row // 2) == (col // 2)), lm, 0.0) for lm in lmats]
    b = 2
    while b < top:
        sub = lower & ((row // (2 * b)) == (col // (2 * b))) & ((row // b) != (col // b))
        cs = [jnp.where(sub, lm, 0.0) for lm in lmats]
        if b % 8:
            ys = [_dot(c, x) for c, x in zip(cs, xs)]
            yield
            xs = [x - _dot(x, y) for x, y in zip(xs, ys)]
            yield
        else:
            zero = jnp.zeros((n, n), F32)
            ys = [_with_odd_blocks(zero, _dot(_odd_blocks(c, b), x), b) for c, x in zip(cs, xs)]
            yield
            xs = [_with_odd_blocks(x, _odd_blocks(x, b) - _dot(_odd_blocks(x, b), y), b) for x, y in zip(xs, ys)]
            yield
        b *= 2
    return xs


def _interleave(main, side):
    side_live = True
    while True:
        if side_live:
            try:
                next(side)
            except StopIteration:
                side_live = False
        try:
            next(main)
        except StopIteration:
            break
    if side_live:
        for _ in side:
            pass


def _const_spec(shape):
    return pl.BlockSpec(shape, lambda *_: (0,) * len(shape), pipeline_mode=pl.Buffered(1))


def _tile_clock(n_t):
    s = pl.program_id(0)
    cur = lax.rem(s, jnp.int32(2))
    t_cur = lax.rem(s, jnp.int32(n_t))
    t_prev = lax.rem(s + jnp.int32(n_t - 1), jnp.int32(n_t))
    return s == 0, t_cur == 0, t_prev == 0, cur, 1 - cur


def _ffn_stages(x_ref, rows, gpre_ref, wg_ref, wu_ref, wd_ref, gpost_ref, gmix_ref, o_ref, hn_ref, a_scr):
    x = x_ref[rows, :]
    h = _rms(x, gpre_ref[...]).astype(BF16)
    yield
    for s, w in FF_CHUNKS:
        g = jnp.dot(h, wg_ref[:, s:s + w], preferred_element_type=F32)
        u = jnp.dot(h, wu_ref[:, s:s + w], preferred_element_type=F32)
        a_scr[rows, s:s + w] = (_silu(g) * u).astype(BF16)
        yield
    f = jnp.dot(a_scr[rows, :], wd_ref[...], preferred_element_type=F32)
    yield
    x1 = x + 0.5 * _rms(f, gpost_ref[...])
    o_ref[rows, :] = x1
    hn_ref[rows, :] = _rms(x1, gmix_ref[...]).astype(BF16)


def _staggered(chains):
    live = []
    waiting = list(chains)
    while waiting or live:
        if waiting:
            live.insert(0, waiting.pop(0))
        for gen in list(live):
            try:
                next(gen)
            except StopIteration:
                live.remove(gen)


def _ffn_kernel(x_ref, gpre_ref, wg_ref, wu_ref, wd_ref, gpost_ref, gmix_ref, o_ref, hn_ref, a_scr):
    part = x_ref.shape[0] // ROW_GROUPS
    _staggered([_ffn_stages(x_ref, slice(i * part, (i + 1) * part), gpre_ref, wg_ref, wu_ref, wd_ref, gpost_ref,
                            gmix_ref, o_ref, hn_ref, a_scr) for i in range(ROW_GROUPS)])


def _ffn(x2d, gpre, wg, wu, wd, gpost, gmix, tm=1024):
    n = x2d.shape[0]
    row = pl.BlockSpec((tm, D_MODEL), lambda i: (i, 0))
    return pl.pallas_call(
        _ffn_kernel,
        out_shape=(jax.ShapeDtypeStruct(x2d.shape, F32), jax.ShapeDtypeStruct(x2d.shape, BF16)),
        grid=(n // tm,),
        in_specs=[row, _const_spec((1, D_MODEL)), _const_spec((D_MODEL, D_FF)), _const_spec((D_MODEL, D_FF)),
                  _const_spec((D_FF, D_MODEL)), _const_spec((1, D_MODEL)), _const_spec((1, D_MODEL))],
        out_specs=(row, row),
        scratch_shapes=[pltpu.VMEM((tm, D_FF), BF16)],
        compiler_params=pltpu.CompilerParams(dimension_semantics=("arbitrary",), vmem_limit_bytes=VMEM_LIMIT),
        name="ffn",
    )(x2d, gpre, wg, wu, wd, gpost, gmix)


def _gdn_kernel(x_ref, win_ref, convw_ref, alog_ref, dtb_ref, nw_ref, y_ref,
                pbuf, q_s, k_s, v_s, bg_s, s_ref, z_scr, zl_scr, u_scr, aqk_scr, kdt_scr, egl_scr, *, n_t):
    tt = x_ref.shape[0]
    c = GDN_CHUNK
    nh = GDN_HEADS
    nc = tt // c
    first_step, seq_start, prev_seq_start, cur, prev = _tile_clock(n_t)

    @pl.when(first_step)
    def _():
        for ref in (z_scr, zl_scr, u_scr, aqk_scr, kdt_scr, egl_scr):
            ref[1] = jnp.zeros(ref.shape[1:], ref.dtype)

    @pl.when(first_step | prev_seq_start)
    def _():
        s_ref[...] = jnp.zeros(s_ref.shape, F32)

    @pl.when(seq_start)
    def _():
        pbuf[0:HALO, :] = jnp.zeros((HALO, pbuf.shape[1]), F32)

    row = lax.broadcasted_iota(jnp.int32, (c, c), 0)
    col = lax.broadcasted_iota(jnp.int32, (c, c), 1)
    incl = row >= col
    strict = row > col
    tri = jnp.where(incl, 1.0, 0.0).astype(BF16)

    def prepare():
        h = x_ref[...]
        z_scr[cur] = jnp.dot(h, win_ref[:, GCOL_Z:GCOL_Z + GDN_W], preferred_element_type=F32)
        pb = jnp.dot(h, win_ref[:, GCOL_BG:GCOL_BG + V7X_LANES], preferred_element_type=F32)
        lane = lax.broadcasted_iota(jnp.int32, (tt, V7X_LANES), 1)
        gdec = -jnp.exp(alog_ref[...]) * _softplus(pb + dtb_ref[...])
        bg_s[...] = jnp.where(lane < nh, _sigmoid(pb), jnp.where(lane < 2 * nh, gdec, 0.0))
        yield
        for grp, dst in ((0, q_s), (1, k_s), (2, v_s)):
            cols = slice(grp * GDN_W, (grp + 1) * GDN_W)
            pbuf[HALO:HALO + tt, cols] = jnp.dot(h, win_ref[:, cols], preferred_element_type=F32)
            for hd in range(nh):
                c0 = grp * GDN_W + hd * GDN_D
                xe = pbuf[0:HALO + tt, c0:c0 + GDN_D]
                acc = xe * convw_ref[0:1, c0:c0 + GDN_D]
                for j in range(1, GDN_CONV):
                    acc = pltpu.roll(acc, 1, axis=0) + xe * convw_ref[j:j + 1, c0:c0 + GDN_D]
                y = _silu(acc[HALO:])
                if grp < 2:
                    y = y * lax.rsqrt(jnp.sum(y * y, axis=-1, keepdims=True) + L2_EPS)
                if grp == 0:
                    y = y * (GDN_D ** -0.5)
                dst[:, hd * GDN_D:(hd + 1) * GDN_D] = y
            yield
        pbuf[0:HALO, :] = pbuf[tt:tt + HALO, :]

        units = []
        for ci in range(nc):
            rows = slice(ci * c, (ci + 1) * c)
            bg = bg_s[rows, :]
            gc = _cumsum_rows(tri, bg)
            gct = gc.T
            for hd in range(nh):
                sl = slice(hd * GDN_D, (hd + 1) * GDN_D)
                gcol = gc[:, nh + hd:nh + hd + 1]
                grow = gct[nh + hd:nh + hd + 1, :]
                glast = gc[c - 1:c, nh + hd:nh + hd + 1]
                beta = bg[:, hd:hd + 1]
                k = k_s[rows, sl]
                units.append(dict(
                    ci=ci, hd=hd, gcol=gcol, glast=glast, k=k, q=q_s[rows, sl], kb=k * beta,
                    vb=v_s[rows, sl] * beta, eg=jnp.exp(gcol),
                    dec=jnp.where(incl, jnp.exp(jnp.where(incl, gcol - grow, 0.0)), 0.0)))
        yield
        kqs = [_dot_nt(jnp.concatenate([un["kb"], un["q"]], axis=0), un["k"]) for un in units]
        yield
        tinvs = yield from _unit_lower_inverses(
            [jnp.where(strict, kq[:c] * un["dec"], 0.0) for kq, un in zip(kqs, units)], c)
        uws = [_dot(tinv, jnp.concatenate([un["vb"], un["kb"] * un["eg"]], axis=1))
               for tinv, un in zip(tinvs, units)]
        yield
        for un, kq, uw in zip(units, kqs, uws):
            ci, hd = un["ci"], un["hd"]
            zl_scr[cur, ci, hd] = jnp.concatenate([uw[:, GDN_D:], un["q"] * un["eg"]], axis=0).astype(BF16)
            u_scr[cur, ci, hd] = uw[:, :GDN_D]
            aqk_scr[cur, ci, hd] = (kq[c:] * un["dec"]).astype(BF16)
            kdt_scr[cur, ci, hd] = (un["k"] * jnp.exp(un["glast"] - un["gcol"])).T.astype(BF16)
            egl_scr[cur, ci, hd] = jnp.broadcast_to(jnp.exp(un["glast"]), (8, GDN_D))

    def finish():
        for ci in range(nc):
            rows = slice(ci * c, (ci + 1) * c)
            zss = [_dot(zl_scr[prev, ci, hd], s_ref[hd]) for hd in range(nh)]
            yield
            v_news = [u_scr[prev, ci, hd] - zss[hd][:c] for hd in range(nh)]
            outs = [zss[hd][c:] + _dot(aqk_scr[prev, ci, hd], v_news[hd]) for hd in range(nh)]
            upds = [_dot(kdt_scr[prev, ci, hd], v_news[hd]) for hd in range(nh)]
            for hd in range(nh):
                s_ref[hd] = s_ref[hd] * egl_scr[prev, ci, hd][0:1, :] + upds[hd]
            for hd in range(nh):
                sl = slice(hd * GDN_D, (hd + 1) * GDN_D)
                o = outs[hd]
                o = o * lax.rsqrt(jnp.mean(o * o, axis=-1, keepdims=True) + NORM_EPS) * nw_ref[...]
                y_ref[rows, sl] = (o * _silu(z_scr[prev, rows, sl])).astype(y_ref.dtype)
            yield

    _interleave(prepare(), finish())


def _flat_tile_specs(n_t, n_tiles, tile, width_in, width_out):
    def in_map(s):
        i = jnp.minimum(s, n_tiles - 1)
        return (i // n_t, i % n_t, 0)

    def out_map(s):
        i = jnp.maximum(s - 1, 0)
        return (i // n_t, i % n_t, 0)

    return pl.BlockSpec((None, tile, width_in), in_map), pl.BlockSpec((None, tile, width_out), out_map)


def _gdn(hn, win, convw, alog, dtb, nw):
    b, t, _ = hn.shape
    tt = GDN_TILE
    nc = tt // GDN_CHUNK
    n_t = t // tt
    consts = (win, convw, alog, dtb, nw)
    in_spec, out_spec = _flat_tile_specs(n_t, b * n_t, tt, D_MODEL, GDN_W)
    return pl.pallas_call(
        functools.partial(_gdn_kernel, n_t=n_t),
        out_shape=jax.ShapeDtypeStruct((b, t, GDN_W), BF16),
        grid=(b * n_t + 1,),
        in_specs=[in_spec] + [_const_spec(c.shape) for c in consts],
        out_specs=out_spec,
        scratch_shapes=[pltpu.VMEM((tt + HALO, GCOL_Z), F32),
                        pltpu.VMEM((tt, GDN_W), F32), pltpu.VMEM((tt, GDN_W), F32), pltpu.VMEM((tt, GDN_W), F32),
                        pltpu.VMEM((tt, V7X_LANES), F32),
                        pltpu.VMEM((GDN_HEADS, GDN_D, GDN_D), F32),
                        pltpu.VMEM((2, tt, GDN_W), F32),
                        pltpu.VMEM((2, nc, GDN_HEADS, 2 * GDN_CHUNK, GDN_D), BF16),
                        pltpu.VMEM((2, nc, GDN_HEADS, GDN_CHUNK, GDN_D), F32),
                        pltpu.VMEM((2, nc, GDN_HEADS, GDN_CHUNK, GDN_CHUNK), BF16),
                        pltpu.VMEM((2, nc, GDN_HEADS, GDN_D, GDN_CHUNK), BF16),
                        pltpu.VMEM((2, nc, GDN_HEADS, 8, GDN_D), F32)],
        compiler_params=pltpu.CompilerParams(dimension_semantics=("arbitrary",), vmem_limit_bytes=VMEM_LIMIT),
        name="gdn",
    )(hn, *consts)


def _rwkv_kernel(x_ref, win_ref, mu_ref, w0_ref, a0_ref, kkw_ref, kaw_ref, w2a2_ref, g2_ref,
                 rk_ref, gnw_ref, gnb_ref, y_ref,
                 pbuf, lw_s, kk_s, a_s, r_s, k_s, v_s, gate_s,
                 s_ref, zl_scr, ut_scr, yv_scr, arb_scr, kbt_scr, vb_scr, dm_scr, *, n_t):
    tt = x_ref.shape[0]
    c = RWKV_CHUNK
    n = RWKV_N
    npair = RWKV_HEADS // 2
    nc = tt // c
    first_step, seq_start, prev_seq_start, cur, prev = _tile_clock(n_t)

    @pl.when(first_step)
    def _():
        for ref in (r_s, k_s, v_s, gate_s, zl_scr, ut_scr, yv_scr, arb_scr, kbt_scr, vb_scr, dm_scr):
            ref[1] = jnp.zeros(ref.shape[1:], ref.dtype)

    @pl.when(first_step | prev_seq_start)
    def _():
        s_ref[...] = jnp.zeros(s_ref.shape, F32)

    @pl.when(seq_start)
    def _():
        pbuf[0:HALO, :] = jnp.zeros((HALO, pbuf.shape[1]), F32)

    lane_t = lax.broadcasted_iota(jnp.int32, (tt, V7X_LANES), 1)
    lo_half = lane_t < n

    def shifted(c0, width):
        now = pbuf[HALO:HALO + tt, c0:c0 + width]
        before = pbuf[HALO - 1:HALO - 1 + tt, c0:c0 + width]
        return now + (before - now) * mu_ref[:, c0:c0 + width]

    def project_group(h, c0, width):
        pbuf[HALO:HALO + tt, c0:c0 + width] = jnp.dot(h, win_ref[:, c0:c0 + width], preferred_element_type=F32)

    def project_and_pointwise():
        h = x_ref[...]
        project_group(h, 3 * RWKV_W, LORA_WA + LORA_G)
        lwa = shifted(3 * RWKV_W, LORA_WA)
        lwa = jnp.where(lo_half, jnp.tanh(lwa), lwa)
        wa = jnp.dot(lwa.astype(BF16), w2a2_ref[...], preferred_element_type=F32)
        lg = shifted(3 * RWKV_W + LORA_WA, LORA_G)
        gate_s[cur] = jnp.dot(_sigmoid(lg).astype(BF16), g2_ref[...], preferred_element_type=F32)
        pairs = [slice(p * V7X_LANES, (p + 1) * V7X_LANES) for p in range(npair)]
        for p, sl in enumerate(pairs):
            w_log = -_softplus(-(w0_ref[:, sl] + wa[:, sl])) - 0.5
            lw_s[:, sl] = -jnp.exp(w_log)
            a_s[:, sl] = _sigmoid(a0_ref[:, sl] + wa[:, RWKV_W + p * V7X_LANES:RWKV_W + (p + 1) * V7X_LANES])
        yield
        project_group(h, 0, RWKV_W)
        for p, sl in enumerate(pairs):
            r_s[cur, :, sl] = shifted(p * V7X_LANES, V7X_LANES)
        yield
        project_group(h, RWKV_W, RWKV_W)
        for p, sl in enumerate(pairs):
            kr = shifted(RWKV_W + p * V7X_LANES, V7X_LANES)
            kx = kr * kkw_ref[:, sl]
            sq = kx * kx
            s_lo = jnp.sum(jnp.where(lo_half, sq, 0.0), axis=-1, keepdims=True)
            s_hi = jnp.sum(jnp.where(lo_half, 0.0, sq), axis=-1, keepdims=True)
            kk_s[:, sl] = kx * lax.rsqrt(jnp.where(lo_half, s_lo, s_hi) + L2_EPS)
            k_s[cur, :, sl] = kr * (1.0 + (a_s[:, sl] - 1.0) * kaw_ref[:, sl])
        yield
        project_group(h, 2 * RWKV_W, RWKV_W)
        for p, sl in enumerate(pairs):
            v_s[cur, :, sl] = shifted(2 * RWKV_W + p * V7X_LANES, V7X_LANES)
        yield
        pbuf[0:HALO, :] = pbuf[tt:tt + HALO, :]

    row_c = lax.broadcasted_iota(jnp.int32, (c, c), 0)
    col_c = lax.broadcasted_iota(jnp.int32, (c, c), 1)
    tri = jnp.where(row_c >= col_c, 1.0, 0.0).astype(BF16)
    row = lax.broadcasted_iota(jnp.int32, (c, 2 * n), 0)
    lane = lax.broadcasted_iota(jnp.int32, (c, 2 * n), 1)
    m0 = lane < n
    jj = jnp.where(m0, lane, lane - n)
    strict = row > jj
    incl = row >= jj
    row2 = lax.broadcasted_iota(jnp.int32, (2 * n, 2 * n), 0)
    lane2 = lax.broadcasted_iota(jnp.int32, (2 * n, 2 * n), 1)
    same_head = (row2 < n) == (lane2 < n)

    def sel0(x):
        return jnp.where(m0, x, 0.0)

    def sel1(x):
        return jnp.where(m0, 0.0, x)

    def head_sum(t):
        s_lo = jnp.sum(sel0(t), axis=-1, keepdims=True)
        s_hi = jnp.sum(sel1(t), axis=-1, keepdims=True)
        return jnp.where(m0, s_lo, s_hi)

    def prepare():
        yield from project_and_pointwise()
        units = []
        for ci in range(nc):
            rows = slice(ci * c, (ci + 1) * c)
            g_all = _cumsum_rows(tri, lw_s[rows, :])
            for p in range(npair):
                sl = slice(p * 2 * n, (p + 1) * 2 * n)
                g = g_all[:, sl]
                gmid = g[c // 2 - 1:c // 2, :]
                gend = g[c - 1:c, :]
                e_inv = jnp.exp(gmid - g)
                e_end = jnp.exp(gend - g)
                egm = jnp.exp(gmid)
                k = k_s[cur, rows, sl]
                kk = kk_s[rows, sl]
                b = kk * a_s[rows, sl]
                kap_t = kk * jnp.exp(g - lw_s[rows, sl] - gmid)
                r_t = r_s[cur, rows, sl] * jnp.exp(g - gmid)
                units.append(dict(ci=ci, p=p, v=v_s[cur, rows, sl], kap_t=kap_t, r_t=r_t, b_t=b * e_inv,
                                  k_t=k * e_inv, kb=jnp.concatenate([k * e_end, -(b * e_end)], axis=0), egm=egm,
                                  egend=jnp.exp(gend)))
        yield
        a0s = [_dot_nt(jnp.concatenate([sel0(un["kap_t"]), sel0(un["r_t"])], axis=0),
                       jnp.concatenate([un["b_t"], un["k_t"]], axis=0)) for un in units]
        yield
        a1s = [_dot_nt(jnp.concatenate([sel1(un["kap_t"]), sel1(un["r_t"])], axis=0),
                       jnp.concatenate([un["k_t"], un["b_t"]], axis=0)) for un in units]
        yield
        lmats = [jnp.concatenate([jnp.where(strict & m0, a0[:c], 0.0), jnp.where(strict & (~m0), a1[:c], 0.0)],
                                 axis=0) for a0, a1 in zip(a0s, a1s)]
        avys = []
        for un, a0, a1 in zip(units, a0s, a1s):
            a_uk = jnp.where(strict, jnp.where(m0, a1[:c], a0[:c]), 0.0)
            a_rk = jnp.where(incl, jnp.where(m0, a1[c:], a0[c:]), 0.0)
            v_sw = jnp.concatenate([sel1(un["v"]), sel0(un["v"])], axis=0)
            avys.append(_dot(jnp.concatenate([a_uk, a_rk], axis=0), v_sw))
        yield
        tbds = yield from _unit_lower_inverses(lmats, c)
        wus = []
        for un, tbd, avy in zip(units, tbds, avys):
            tcat = tbd[:c] + tbd[c:]
            kap_b = un["kap_t"] * un["egm"]
            av = avy[:c]
            x = jnp.concatenate([jnp.concatenate([sel0(kap_b), sel0(av)], axis=1),
                                 jnp.concatenate([sel1(kap_b), sel1(av)], axis=1)], axis=0)
            wus.append(_dot(tcat, x))
        yield
        for un, a0, a1, avy, wu in zip(units, a0s, a1s, avys, wus):
            ci, p = un["ci"], un["p"]
            zl_scr[cur, ci, p] = jnp.concatenate([wu[:, :2 * n], un["r_t"] * un["egm"]], axis=0).astype(BF16)
            ut_scr[cur, ci, p] = wu[:, 2 * n:]
            yv_scr[cur, ci, p] = avy[c:]
            arb_scr[cur, ci, p] = jnp.where(incl, jnp.where(m0, a0[c:], a1[c:]), 0.0).astype(BF16)
            kbt_scr[cur, ci, p] = un["kb"].T.astype(BF16)
            vb_scr[cur, ci, p] = un["v"].astype(BF16)
            dm_scr[cur, ci, p] = jnp.broadcast_to(un["egend"], (2 * n, 2 * n)).T

    def finish():
        for ci in range(nc):
            rows = slice(ci * c, (ci + 1) * c)
            zss = [_dot(zl_scr[prev, ci, p], s_ref[p]) for p in range(npair)]
            yield
            us = [zss[p][:c] + ut_scr[prev, ci, p] for p in range(npair)]
            ys = [zss[p][c:] - _dot(arb_scr[prev, ci, p], jnp.concatenate([sel0(us[p]), sel1(us[p])], axis=0))
                  + yv_scr[prev, ci, p] for p in range(npair)]
            upds = [jnp.dot(kbt_scr[prev, ci, p], jnp.concatenate([vb_scr[prev, ci, p], us[p].astype(BF16)], axis=0),
                            preferred_element_type=F32) for p in range(npair)]
            for p in range(npair):
                s_ref[p] = s_ref[p] * dm_scr[prev, ci, p] + jnp.where(same_head, upds[p], 0.0)
            for p in range(npair):
                sl = slice(p * 2 * n, (p + 1) * 2 * n)
                y = ys[p]
                mean = head_sum(y) * (1.0 / n)
                dlt = y - mean
                var = head_sum(dlt * dlt) * (1.0 / n)
                yn = dlt * lax.rsqrt(var + GN_EPS) * gnw_ref[:, sl] + gnb_ref[:, sl]
                bonus = head_sum(r_s[prev, rows, sl] * k_s[prev, rows, sl] * rk_ref[:, sl]) * v_s[prev, rows, sl]
                y_ref[rows, sl] = ((yn + bonus) * gate_s[prev, rows, sl]).astype(y_ref.dtype)
            yield

    _interleave(prepare(), finish())


def _rwkv(hn, win, mu, w0, a0, k_k, k_a, w2a2, g2, r_k, gn_w, gn_b):
    b, t, _ = hn.shape
    tt = RWKV_TILE
    nc = tt // RWKV_CHUNK
    n_t = t // tt
    npair = RWKV_HEADS // 2
    slab = 2 * RWKV_N
    consts = (win, mu, w0, a0, k_k, k_a, w2a2, g2, r_k, gn_w, gn_b)
    wide = pltpu.VMEM((tt, RWKV_W), F32)
    wide2 = pltpu.VMEM((2, tt, RWKV_W), F32)
    in_spec, out_spec = _flat_tile_specs(n_t, b * n_t, tt, D_MODEL, RWKV_W)
    return pl.pallas_call(
        functools.partial(_rwkv_kernel, n_t=n_t),
        out_shape=jax.ShapeDtypeStruct((b, t, RWKV_W), BF16),
        grid=(b * n_t + 1,),
        in_specs=[in_spec] + [_const_spec(c.shape) for c in consts],
        out_specs=out_spec,
        scratch_shapes=[pltpu.VMEM((tt + HALO, RW_COLS), F32), wide, wide, wide, wide2, wide2, wide2, wide2,
                        pltpu.VMEM((npair, slab, slab), F32),
                        pltpu.VMEM((2, nc, npair, 2 * RWKV_CHUNK, slab), BF16),
                        pltpu.VMEM((2, nc, npair, RWKV_CHUNK, slab), F32),
                        pltpu.VMEM((2, nc, npair, RWKV_CHUNK, slab), F32),
                        pltpu.VMEM((2, nc, npair, RWKV_CHUNK, slab), BF16),
                        pltpu.VMEM((2, nc, npair, slab, 2 * RWKV_CHUNK), BF16),
                        pltpu.VMEM((2, nc, npair, RWKV_CHUNK, slab), BF16),
                        pltpu.VMEM((2, nc, npair, slab, slab), F32)],
        compiler_params=pltpu.CompilerParams(dimension_semantics=("arbitrary",), vmem_limit_bytes=VMEM_LIMIT),
        name="rwkv",
    )(hn, *consts)


def _tail_stages(rows, x_ref, yg_ref, yr_ref, p_ref, wout_ref, gmp_ref, gpre_ref, wg_ref, wu_ref, wd_ref, gpost_ref,
                 gple_ref, wpg_ref, wpp_ref, gplep_ref, o_ref, a_scr):
    m = (jnp.dot(yg_ref[rows, :], wout_ref[0:GDN_W, :], preferred_element_type=F32)
         + jnp.dot(yr_ref[rows, :], wout_ref[GDN_W:GDN_W + RWKV_W, :], preferred_element_type=F32))
    yield
    x = x_ref[rows, :] + _rms(m, gmp_ref[...])
    h = _rms(x, gpre_ref[...]).astype(BF16)
    yield
    for s, w in FF_CHUNKS:
        g = jnp.dot(h, wg_ref[:, s:s + w], preferred_element_type=F32)
        u = jnp.dot(h, wu_ref[:, s:s + w], preferred_element_type=F32)
        a_scr[rows, s:s + w] = (_silu(g) * u).astype(BF16)
        yield
    f = jnp.dot(a_scr[rows, :], wd_ref[...], preferred_element_type=F32)
    yield
    x = x + 0.5 * _rms(f, gpost_ref[...])
    gate = _sigmoid(jnp.dot(_rms(x, gple_ref[...]).astype(BF16), wpg_ref[...], preferred_element_type=F32))
    emb = jnp.dot(p_ref[rows, :].astype(BF16), wpp_ref[...], preferred_element_type=F32)
    yield
    o_ref[rows, :] = x + _rms(gate * emb, gplep_ref[...])


def _tail_kernel(x_ref, *refs):
    part = x_ref.shape[0] // ROW_GROUPS
    _staggered([_tail_stages(slice(i * part, (i + 1) * part), x_ref, *refs) for i in range(ROW_GROUPS)])


def _tail(x1, yg, yr, p, wout, gmp, gpre, wg, wu, wd, gpost, gple, wpg, wpp, gplep, tm=1024):
    n = x1.shape[0]
    row = pl.BlockSpec((tm, D_MODEL), lambda i: (i, 0))
    half = pl.BlockSpec((tm, GDN_W), lambda i: (i, 0))
    ple = pl.BlockSpec((tm, D_PLE), lambda i: (i, 0))
    consts = (wout, gmp, gpre, wg, wu, wd, gpost, gple, wpg, wpp, gplep)
    return pl.pallas_call(
        _tail_kernel,
        out_shape=jax.ShapeDtypeStruct(x1.shape, F32),
        grid=(n // tm,),
        in_specs=[row, half, half, ple] + [_const_spec(c.shape) for c in consts],
        out_specs=row,
        scratch_shapes=[pltpu.VMEM((tm, D_FF), BF16)],
        compiler_params=pltpu.CompilerParams(dimension_semantics=("arbitrary",), vmem_limit_bytes=VMEM_LIMIT),
        name="tail",
    )(x1, yg, yr, p, *consts)


def _row(v):
    return v.reshape(1, -1).astype(F32)


def _layer(x, p, ffn1_norm_pre, ffn1_w_gate, ffn1_w_up, ffn1_w_down, ffn1_norm_post, mix_norm_pre, w_in,
           gdn_conv_w, gdn_a_log, gdn_dt_bias, gdn_norm_w, rwkv_mu, rwkv_w0, rwkv_w2, rwkv_a0, rwkv_a2, rwkv_g2,
           rwkv_k_k, rwkv_k_a, rwkv_r_k, rwkv_gn_w, rwkv_gn_b, w_out, mix_norm_post, ffn2_norm_pre, ffn2_w_gate,
           ffn2_w_up, ffn2_w_down, ffn2_norm_post, ple_norm_pre, ple_w_gate, ple_w_proj, ple_norm_post):
    b, t, d = x.shape
    n = b * t
    x1, hn = _ffn(x.reshape(n, d), _row(ffn1_norm_pre), ffn1_w_gate.astype(BF16), ffn1_w_up.astype(BF16),
                  ffn1_w_down.astype(BF16), _row(ffn1_norm_post), _row(mix_norm_pre))
    hn = hn.reshape(b, t, d)

    n_main = 4 * GDN_W
    n_bg = 2 * GDN_HEADS
    win_gdn = jnp.concatenate([w_in[:, :n_main + n_bg], jnp.zeros((d, V7X_LANES - n_bg), w_in.dtype)],
                              axis=1).astype(BF16)
    win_rw = w_in[:, n_main + n_bg:].astype(BF16)
    pad_bg = lambda vec: jnp.concatenate([jnp.zeros((GDN_HEADS,), F32), vec.astype(F32),
                                          jnp.zeros((V7X_LANES - n_bg,), F32)]).reshape(1, V7X_LANES)
    zero_l = jnp.zeros((RWKV_N, RWKV_W), F32)
    w2a2 = jnp.concatenate([jnp.concatenate([rwkv_w2.astype(F32), zero_l], axis=1),
                            jnp.concatenate([zero_l, rwkv_a2.astype(F32)], axis=1)], axis=0).astype(BF16)
    y_gdn = _gdn(hn, win_gdn, gdn_conv_w.astype(F32), pad_bg(gdn_a_log), pad_bg(gdn_dt_bias), _row(gdn_norm_w))
    y_rwkv = _rwkv(hn, win_rw, _row(rwkv_mu), _row(rwkv_w0), _row(rwkv_a0), _row(rwkv_k_k), _row(rwkv_k_a),
                   w2a2, rwkv_g2.astype(BF16), _row(rwkv_r_k), _row(rwkv_gn_w), _row(rwkv_gn_b))
    out = _tail(x1, y_gdn.reshape(n, GDN_W), y_rwkv.reshape(n, RWKV_W), p.reshape(n, D_PLE),
                w_out.astype(BF16), _row(mix_norm_post), _row(ffn2_norm_pre), ffn2_w_gate.astype(BF16),
                ffn2_w_up.astype(BF16), ffn2_w_down.astype(BF16), _row(ffn2_norm_post), _row(ple_norm_pre),
                ple_w_gate.astype(BF16), ple_w_proj.astype(BF16), _row(ple_norm_post))
    return out.reshape(b, t, d)


def kernel(x, p, ffn1_norm_pre, ffn1_w_gate, ffn1_w_up, ffn1_w_down, ffn1_norm_post, mix_norm_pre, w_in, gdn_conv_w, gdn_a_log, gdn_dt_bias, gdn_norm_w, rwkv_mu, rwkv_w0, rwkv_w2, rwkv_a0, rwkv_a2, rwkv_g2, rwkv_k_k, rwkv_k_a, rwkv_r_k, rwkv_gn_w, rwkv_gn_b, w_out, mix_norm_post, ffn2_norm_pre, ffn2_w_gate, ffn2_w_up, ffn2_w_down, ffn2_norm_post, ple_norm_pre, ple_w_gate, ple_w_proj, ple_norm_post):
    return _layer(x, p[0], ffn1_norm_pre[0], ffn1_w_gate[0], ffn1_w_up[0], ffn1_w_down[0], ffn1_norm_post[0],
                  mix_norm_pre[0], w_in[0], gdn_conv_w[0], gdn_a_log[0], gdn_dt_bias[0], gdn_norm_w[0], rwkv_mu[0],
                  rwkv_w0[0], rwkv_w2[0], rwkv_a0[0], rwkv_a2[0], rwkv_g2[0], rwkv_k_k[0], rwkv_k_a[0], rwkv_r_k[0],
                  rwkv_gn_w[0], rwkv_gn_b[0], w_out[0], mix_norm_post[0], ffn2_norm_pre[0], ffn2_w_gate[0],
                  ffn2_w_up[0], ffn2_w_down[0], ffn2_norm_post[0], ple_norm_pre[0], ple_w_gate[0], ple_w_proj[0],
                  ple_norm_post[0])
```
